```python
import jax
import jax.numpy as jnp
from jax import lax
import numpy as np

D_MODEL = 1024
BATCH = 16
SEQ = 2048
DEPTH = 2
DEC_BATCH = 128
DEC_SEQ = 4
PAST_LEN = 8192
PAGE_SIZE = 128

N_EVEN = (DEPTH + 1) // 2
N_ODD = DEPTH // 2

A_HEADS = 8
A_HEAD_DIM = 64
A_WIDTH = A_HEADS * A_HEAD_DIM
A_PATTERNS = ((128, 1), (512, 4), (2048, 16))
A_MAX_WINDOW = 2048
BAND_BLOCK = 128
B_HEADS = 4
B_KEY_DIM = 128
B_VAL_DIM = 128
B_QK_WIDTH = B_HEADS * B_KEY_DIM
B_V_WIDTH = B_HEADS * B_VAL_DIM
RET_CHUNK = 128
IN_AB_WIDTH = 3 * A_WIDTH + 2 * B_QK_WIDTH + 2 * B_V_WIDTH
MIX_AB_WIDTH = A_WIDTH + B_V_WIDTH
C_HEADS = 8
C_Q_LORA = 384
C_KV_LORA = 256
C_NOPE = 128
C_ROPE = 64
C_V_DIM = 128
Q_BLOCK = 128
MLA_SCALE = (C_NOPE + C_ROPE) ** -0.5
D_FF = 2816
CONV_WIDTH = 3
ROPE_BASE = 10000.0
EPS = 1e-6
F32 = jnp.float32

kernel_name = 'hybrid_dilated_retention_mla_convffn_step'


def rmsnorm(x, g):
    xf = x.astype(F32)
    y = xf * lax.rsqrt(jnp.mean(xf * xf, axis=-1, keepdims=True) + EPS)
    return (y * g.astype(F32)).astype(x.dtype)


def rope(x, pos):
    half = x.shape[-1] // 2
    inv = ROPE_BASE ** (-jnp.arange(half, dtype=F32) / half)
    ang = pos.astype(F32)[:, None] * inv[None, :]
    cos = jnp.cos(ang)[:, None, :]
    sin = jnp.sin(ang)[:, None, :]
    x1 = x[..., :half].astype(F32)
    x2 = x[..., half:].astype(F32)
    return jnp.concatenate([x1 * cos - x2 * sin, x2 * cos + x1 * sin], axis=-1).astype(x.dtype)


def alibi_slopes():
    return 2.0 ** (-8.0 * (jnp.arange(A_HEADS, dtype=F32) + 1.0) / A_HEADS)


def retention_log_decay():
    return jnp.log(1.0 - 2.0 ** (-5.0 - jnp.arange(B_HEADS, dtype=F32)))


def dilated_band_prompt(q, k, v, slopes, window, dil):
    B, S, H, Dh = q.shape
    L = S // dil
    nb = -(-L // BAND_BLOCK)
    Lp = nb * BAND_BLOCK
    reach = window // dil

    def streams(a, front):
        a = a.astype(F32).reshape(B, L, dil, H, Dh).transpose(0, 2, 1, 3, 4)
        return jnp.pad(a, ((0, 0), (0, 0), (front, Lp - L), (0, 0), (0, 0)))

    def key_blocks(a):
        a = streams(a, BAND_BLOCK).reshape(B, dil, nb + 1, BAND_BLOCK, H, Dh)
        return jnp.concatenate([a[:, :, :-1], a[:, :, 1:]], axis=3)

    qs = streams(q, 0).reshape(B, dil, nb, BAND_BLOCK, H, Dh)
    kb = key_blocks(k)
    vb = key_blocks(v)
    start = jnp.arange(nb)[:, None] * BAND_BLOCK
    qj = start + jnp.arange(BAND_BLOCK)[None, :]
    kj = start - BAND_BLOCK + jnp.arange(2 * BAND_BLOCK)[None, :]
    rel = qj[:, :, None] - kj[:, None, :]
    valid = (rel >= 0) & (rel <= reach) & (kj[:, None, :] >= 0)
    bias = -slopes[None, :, None, None] * (rel * dil).astype(F32)[:, None]
    s = jnp.einsum('bdnqhe,bdnkhe->bdnhqk', qs, kb) * (Dh ** -0.5) + bias
    s = jnp.where(valid[:, None], s, -jnp.inf)
    lse = jax.nn.logsumexp(s, axis=-1)
    p = jnp.exp(s - lse[..., None])
    o = jnp.einsum('bdnhqk,bdnkhe->bdnqhe', p, vb)
    o = o.reshape(B, dil, Lp, H, Dh)[:, :, :L].transpose(0, 2, 1, 3, 4).reshape(B, S, H, Dh)
    lse = lse.transpose(0, 1, 2, 4, 3).reshape(B, dil, Lp, H)[:, :, :L]
    lse = lse.transpose(0, 2, 1, 3).reshape(B, S, H)
    return o, lse


def dilated_window_decode(q, k_all, v_all, slopes, window, dil):
    B, T, H, Dh = q.shape
    nbuf = k_all.shape[1] - T
    offs = jnp.arange(window // dil + 1) * dil
    idx = nbuf + jnp.arange(T)[:, None] - offs[None, :]
    valid = idx >= 0
    idx = jnp.maximum(idx, 0)
    kg = jnp.take(k_all, idx, axis=1).astype(F32)
    vg = jnp.take(v_all, idx, axis=1).astype(F32)
    s = jnp.einsum('bthe,btkhe->bhtk', q.astype(F32), kg) * (Dh ** -0.5)
    s = s - slopes[:, None, None] * offs.astype(F32)[None, None, :]
    s = jnp.where(valid[None, None], s, -jnp.inf)
    lse = jax.nn.logsumexp(s, axis=-1)
    p = jnp.exp(s - lse[..., None])
    o = jnp.einsum('bhtk,btkhe->bthe', p, vg)
    return o, lse.transpose(0, 2, 1)


def combine_patterns(parts):
    outs = jnp.stack([o for o, _ in parts], axis=0)
    lses = jnp.stack([l for _, l in parts], axis=0)
    w = jax.nn.softmax(lses, axis=0)
    return jnp.sum(w[..., None] * outs, axis=0)


def retention_scan(q, k, v, s0, chunk):
    B, S, H, Dk = q.shape
    Dv = v.shape[-1]
    nc = S // chunk
    lg = retention_log_decay()
    i = jnp.arange(chunk, dtype=F32)
    rel = i[:, None] - i[None, :]
    dmask = jnp.where(rel >= 0, jnp.exp(jnp.maximum(rel, 0.0)[None] * lg[:, None, None]), 0.0)
    q_dec = jnp.exp((i + 1.0)[:, None] * lg[None, :])
    k_dec = jnp.exp((chunk - 1.0 - i)[:, None] * lg[None, :])
    c_dec = jnp.exp(chunk * lg)

    def to_chunks(a):
        return a.astype(F32).reshape(B, nc, chunk, H, a.shape[-1]).transpose(1, 0, 2, 3, 4)

    def step(s, inp):
        qc, kc, vc = inp
        att = jnp.einsum('bihd,bjhd->bhij', qc, kc) * dmask
        o = jnp.einsum('bhij,bjhe->bihe', att, vc)
        o = o + jnp.einsum('bihd,bhde->bihe', qc, s) * q_dec[None, :, :, None]
        s = s * c_dec[None, :, None, None] + jnp.einsum('bjhd,bjhe->bhde', kc * k_dec[None, :, :, None], vc)
        return s, o

    s, o = lax.scan(step, s0.astype(F32), (to_chunks(q), to_chunks(k), to_chunks(v)))
    return o.transpose(1, 0, 2, 3, 4).reshape(B, S, H, Dv), s


def head_norm(o, g):
    B, S = o.shape[0], o.shape[1]
    mu = jnp.mean(o, axis=-1, keepdims=True)
    d = o - mu
    var = jnp.mean(d * d, axis=-1, keepdims=True)
    return (d * lax.rsqrt(var + EPS)).reshape(B, S, -1) * g.astype(F32)


def mixer_ab(h, pos, w_in, w_out, g_ret, ret_state, k_buf, v_buf):
    B, S, _ = h.shape
    proj = h @ w_in
    cuts = [A_WIDTH, 2 * A_WIDTH, 3 * A_WIDTH, 3 * A_WIDTH + B_QK_WIDTH,
            3 * A_WIDTH + 2 * B_QK_WIDTH, 3 * A_WIDTH + 2 * B_QK_WIDTH + B_V_WIDTH]
    qa, ka, va, qb, kb, vb, gb = jnp.split(proj, cuts, axis=-1)
    qa = qa.reshape(B, S, A_HEADS, A_HEAD_DIM)
    ka = ka.reshape(B, S, A_HEADS, A_HEAD_DIM)
    va = va.reshape(B, S, A_HEADS, A_HEAD_DIM)
    slopes = alibi_slopes()
    if k_buf is None:
        parts = [dilated_band_prompt(qa, ka, va, slopes, w, d) for (w, d) in A_PATTERNS]
        keep = min(A_MAX_WINDOW, S)
        new_k = ka[:, S - keep:]
        new_v = va[:, S - keep:]
    else:
        k_all = jnp.concatenate([k_buf.astype(ka.dtype), ka], axis=1)
        v_all = jnp.concatenate([v_buf.astype(va.dtype), va], axis=1)
        parts = [dilated_window_decode(qa, k_all, v_all, slopes, w, d) for (w, d) in A_PATTERNS]
        keep = k_buf.shape[1]
        new_k = k_all[:, k_all.shape[1] - keep:]
        new_v = v_all[:, v_all.shape[1] - keep:]
    o_a = combine_patterns(parts).reshape(B, S, A_WIDTH)
    qb = rope(qb.reshape(B, S, B_HEADS, B_KEY_DIM), pos)
    kb = rope(kb.reshape(B, S, B_HEADS, B_KEY_DIM), pos) * (B_KEY_DIM ** -0.5)
    vb = vb.reshape(B, S, B_HEADS, B_VAL_DIM)
    chunk = RET_CHUNK if S % RET_CHUNK == 0 else S
    o_b, new_s = retention_scan(qb, kb, vb, ret_state, chunk)
    o_b = head_norm(o_b, g_ret) * jax.nn.silu(gb.astype(F32))
    mix = jnp.concatenate([o_a, o_b], axis=-1).astype(h.dtype)
    return mix @ w_out, new_k, new_v, new_s


def mla_scores(q_lat, q_rope, c, kr):
    return (jnp.einsum('bthc,bkc->bhtk', q_lat, c) + jnp.einsum('bthr,bkr->bhtk', q_rope, kr)) * MLA_SCALE


def mixer_c(h, pos, w_dq, g_q, w_uq, w_dkv, g_kv, w_uk, w_uv, w_o, c_pool, kr_pool, page_table):
    B, S, _ = h.shape
    cq = rmsnorm(h @ w_dq, g_q)
    q = (cq @ w_uq).reshape(B, S, C_HEADS, C_NOPE + C_ROPE)
    q_nope = q[..., :C_NOPE].astype(F32)
    q_rope = rope(q[..., C_NOPE:], pos).astype(F32)
    kv = h @ w_dkv
    c_kv = rmsnorm(kv[..., :C_KV_LORA], g_kv)
    k_rope = rope(kv[..., None, C_KV_LORA:], pos)[:, :, 0]
    q_lat = jnp.einsum('bshn,chn->bshc', q_nope, w_uk.astype(F32))
    cf = c_kv.astype(F32)
    krf = k_rope.astype(F32)
    if c_pool is None:
        nb = S // Q_BLOCK
        qlb = q_lat.reshape(B, nb, Q_BLOCK, C_HEADS, C_KV_LORA).transpose(1, 0, 2, 3, 4)
        qrb = q_rope.reshape(B, nb, Q_BLOCK, C_HEADS, C_ROPE).transpose(1, 0, 2, 3, 4)
        kpos = jnp.arange(S)

        def attend_block(args):
            ql, qr, start = args
            s = mla_scores(ql, qr, cf, krf)
            qpos = start + jnp.arange(Q_BLOCK)
            s = jnp.where((kpos[None, :] <= qpos[:, None])[None, None], s, -jnp.inf)
            p = jax.nn.softmax(s, axis=-1)
            return jnp.einsum('bhtk,bkc->bthc', p, cf)

        o_lat = lax.map(attend_block, (qlb, qrb, jnp.arange(nb) * Q_BLOCK))
        o_lat = o_lat.transpose(1, 0, 2, 3, 4).reshape(B, S, C_HEADS, C_KV_LORA)
    else:
        past_c = c_pool[page_table].reshape(B, -1, C_KV_LORA).astype(F32)
        past_kr = kr_pool[page_table].reshape(B, -1, C_ROPE).astype(F32)
        npast = past_c.shape[1]
        s_past = mla_scores(q_lat, q_rope, past_c, past_kr)
        s_new = mla_scores(q_lat, q_rope, cf, krf)
        tri = jnp.tril(jnp.ones((S, S), dtype=bool))
        s_new = jnp.where(tri[None, None], s_new, -jnp.inf)
        p = jax.nn.softmax(jnp.concatenate([s_past, s_new], axis=-1), axis=-1)
        o_lat = (jnp.einsum('bhtk,bkc->bthc', p[..., :npast], past_c)
                 + jnp.einsum('bhtk,bkc->bthc', p[..., npast:], cf))
    o = jnp.einsum('bshc,chv->bshv', o_lat, w_uv.astype(F32)).reshape(B, S, C_HEADS * C_V_DIM)
    return o.astype(h.dtype) @ w_o, c_kv, k_rope


def conv_ffn(h, conv_state, w_up, w_conv, b_conv, w_down):
    S = h.shape[1]
    u = h @ w_up
    gate = u[..., :D_FF]
    up = u[..., D_FF:]
    ext = jnp.concatenate([conv_state.astype(gate.dtype), gate], axis=1)
    conv = ext[:, 0:S] * w_conv[0] + b_conv
    for tap in range(1, CONV_WIDTH):
        conv = conv + ext[:, tap:tap + S] * w_conv[tap]
    act = jax.nn.silu(conv.astype(F32)) * up.astype(F32)
    return act.astype(h.dtype) @ w_down, ext[:, S:]


def run_trunk(x, pos, params, caches):
    (g_mix, g_ffn, g_final, w_in_ab, w_out_ab, g_ret, w_dq, g_q, w_uq, w_dkv, g_kv,
     w_uk, w_uv, w_o_c, w_up, w_conv, b_conv, w_down) = params
    B = x.shape[0]
    prompt = caches is None
    if not prompt:
        a_k, a_v, ret, conv_buf, c_pool, kr_pool, page_table = caches
    new_ak, new_av, new_ret, new_c, new_kr, new_conv = [], [], [], [], [], []
    ie = 0
    io = 0
    for layer in range(DEPTH):
        h = rmsnorm(x, g_mix[layer])
        if layer % 2 == 0:
            if prompt:
                s0 = jnp.zeros((B, B_HEADS, B_KEY_DIM, B_VAL_DIM), F32)
                y, nk, nv, ns = mixer_ab(h, pos, w_in_ab[ie], w_out_ab[ie], g_ret[ie], s0, None, None)
            else:
                y, nk, nv, ns = mixer_ab(h, pos, w_in_ab[ie], w_out_ab[ie], g_ret[ie], ret[ie], a_k[ie], a_v[ie])
            new_ak.append(nk)
            new_av.append(nv)
            new_ret.append(ns)
            ie += 1
        else:
            if prompt:
                y, nc, nkr = mixer_c(h, pos, w_dq[io], g_q[io], w_uq[io], w_dkv[io], g_kv[io],
                                     w_uk[io], w_uv[io], w_o_c[io], None, None, None)
            else:
                y, nc, nkr = mixer_c(h, pos, w_dq[io], g_q[io], w_uq[io], w_dkv[io], g_kv[io],
                                     w_uk[io], w_uv[io], w_o_c[io], c_pool[io], kr_pool[io], page_table)
            new_c.append(nc)
            new_kr.append(nkr)
            io += 1
        x = x + y
        h = rmsnorm(x, g_ffn[layer])
        cs = jnp.zeros((B, CONV_WIDTH - 1, D_FF), x.dtype) if prompt else conv_buf[layer]
        y, ncv = conv_ffn(h, cs, w_up[layer], w_conv[layer], b_conv[layer], w_down[layer])
        new_conv.append(ncv)
        x = x + y
    x = rmsnorm(x, g_final)
    return x, (jnp.stack(new_ak), jnp.stack(new_av), jnp.stack(new_ret),
               jnp.stack(new_c), jnp.stack(new_kr), jnp.stack(new_conv))


def setup_inputs(seed: int = 0) -> dict:
    key = jax.random.key(seed)
    ks = jax.random.split(key, 40)

    def nrm(k, shape, scale):
        return jax.random.normal(k, shape, F32) * scale

    n_pages = PAST_LEN // PAGE_SIZE
    n_used = DEC_BATCH * n_pages
    n_phys = n_used + max(1, n_used // 4)
    a_buf = min(A_MAX_WINDOW, PAST_LEN)
    perm = jax.random.permutation(ks[0], n_phys).astype(jnp.int32)
    page_table = perm[:n_used].reshape(DEC_BATCH, n_pages)
    return {
        'x_prompt': nrm(ks[1], (BATCH, SEQ, D_MODEL), 1.0),
        'x_sample': nrm(ks[2], (DEC_BATCH, DEC_SEQ, D_MODEL), 1.0),
        'cache_a_k': nrm(ks[3], (N_EVEN, DEC_BATCH, a_buf, A_HEADS, A_HEAD_DIM), 1.0),
        'cache_a_v': nrm(ks[4], (N_EVEN, DEC_BATCH, a_buf, A_HEADS, A_HEAD_DIM), 1.0),
        'state_ret': nrm(ks[5], (N_EVEN, DEC_BATCH, B_HEADS, B_KEY_DIM, B_VAL_DIM), 1.0),
        'cache_c_kv': nrm(ks[6], (N_ODD, n_phys, PAGE_SIZE, C_KV_LORA), 1.0),
        'cache_k_rope': nrm(ks[7], (N_ODD, n_phys, PAGE_SIZE, C_ROPE), 1.0),
        'state_ffn_conv': nrm(ks[8], (DEPTH, DEC_BATCH, CONV_WIDTH - 1, D_FF), 1.0),
        'page_table': page_table,
        'g_mix': 1.0 + nrm(ks[9], (DEPTH, D_MODEL), 0.01),
        'g_ffn': 1.0 + nrm(ks[10], (DEPTH, D_MODEL), 0.01),
        'g_final': 1.0 + nrm(ks[11], (D_MODEL,), 0.01),
        'w_in_ab': nrm(ks[12], (N_EVEN, D_MODEL, IN_AB_WIDTH), D_MODEL ** -0.5),
        'w_out_ab': nrm(ks[13], (N_EVEN, MIX_AB_WIDTH, D_MODEL), MIX_AB_WIDTH ** -0.5),
        'g_ret': 1.0 + nrm(ks[14], (N_EVEN, B_V_WIDTH), 0.01),
        'w_dq': nrm(ks[15], (N_ODD, D_MODEL, C_Q_LORA), D_MODEL ** -0.5),
        'g_q': 1.0 + nrm(ks[16], (N_ODD, C_Q_LORA), 0.01),
        'w_uq': nrm(ks[17], (N_ODD, C_Q_LORA, C_HEADS * (C_NOPE + C_ROPE)), C_Q_LORA ** -0.5),
        'w_dkv': nrm(ks[18], (N_ODD, D_MODEL, C_KV_LORA + C_ROPE), D_MODEL ** -0.5),
        'g_kv': 1.0 + nrm(ks[19], (N_ODD, C_KV_LORA), 0.01),
        'w_uk': nrm(ks[20], (N_ODD, C_KV_LORA, C_HEADS, C_NOPE), C_KV_LORA ** -0.5),
        'w_uv': nrm(ks[21], (N_ODD, C_KV_LORA, C_HEADS, C_V_DIM), C_KV_LORA ** -0.5),
        'w_o_c': nrm(ks[22], (N_ODD, C_HEADS * C_V_DIM, D_MODEL), (C_HEADS * C_V_DIM) ** -0.5),
        'w_up': nrm(ks[23], (DEPTH, D_MODEL, 2 * D_FF), D_MODEL ** -0.5),
        'w_conv': nrm(ks[24], (DEPTH, CONV_WIDTH, D_FF), CONV_WIDTH ** -0.5),
        'b_conv': nrm(ks[25], (DEPTH, D_FF), 0.01),
        'w_down': nrm(ks[26], (DEPTH, D_FF, D_MODEL), D_FF ** -0.5),
    }


def reference(x_prompt, x_sample, cache_a_k, cache_a_v, state_ret, cache_c_kv, cache_k_rope,
              state_ffn_conv, page_table, g_mix, g_ffn, g_final, w_in_ab, w_out_ab, g_ret,
              w_dq, g_q, w_uq, w_dkv, g_kv, w_uk, w_uv, w_o_c, w_up, w_conv, b_conv, w_down):
    params = (g_mix, g_ffn, g_final, w_in_ab, w_out_ab, g_ret, w_dq, g_q, w_uq, w_dkv, g_kv,
              w_uk, w_uv, w_o_c, w_up, w_conv, b_conv, w_down)
    pos_p = jnp.arange(x_prompt.shape[1])
    pos_s = PAST_LEN + jnp.arange(x_sample.shape[1])
    y_prompt, st_p = run_trunk(x_prompt, pos_p, params, None)
    y_sample, st_s = run_trunk(x_sample, pos_s, params,
                               (cache_a_k, cache_a_v, state_ret, state_ffn_conv,
                                cache_c_kv, cache_k_rope, page_table))
    new_a_k_p, new_a_v_p, new_ret_p, new_c_kv_p, new_k_rope_p, new_conv_p = st_p
    new_a_k_s, new_a_v_s, new_ret_s, new_c_kv_s, new_k_rope_s, new_conv_s = st_s
    return (y_prompt, y_sample, new_a_k_p, new_a_v_p, new_ret_p, new_c_kv_p, new_k_rope_p, new_conv_p,
            new_a_k_s, new_a_v_s, new_ret_s, new_c_kv_s, new_k_rope_s, new_conv_s)
```

```python
import functools

import jax
import jax.numpy as jnp
from jax import lax
from jax.experimental import pallas as pl
from jax.experimental.pallas import tpu as pltpu

F32 = jnp.float32
BF16 = jnp.bfloat16

D_MODEL = 1024
A_HEADS = 8
A_HEAD_DIM = 64
A_WIDTH = A_HEADS * A_HEAD_DIM
A_PATTERNS = ((128, 1), (512, 4), (2048, 16))
BAND = 128
B_HEADS = 4
B_DIM = 128
B_WIDTH = B_HEADS * B_DIM
RET_CHUNK = 128
C_HEADS = 8
C_Q_LORA = 384
C_KV_LORA = 256
C_NOPE = 128
C_ROPE = 64
C_V_DIM = 128
C_CAT = C_KV_LORA + 128
MLA_SCALE = (C_NOPE + C_ROPE) ** -0.5
D_FF = 2816
ROPE_BASE = 10000.0
EPS = 1e-6
PAST_LEN = 8192
PAGE = 128

VMEM_LIMIT = 56 * 1024 * 1024

NT_DIMS = (((1,), (1,)), ((), ()))
TN_DIMS = (((0,), (0,)), ((), ()))


def _params(*sem):
    return pltpu.CompilerParams(dimension_semantics=sem, vmem_limit_bytes=VMEM_LIMIT)


def _resident(shape):
    nd = len(shape)
    return pl.BlockSpec(shape, lambda *_: (0,) * nd, pipeline_mode=pl.Buffered(1))


def _rms(x, g):
    return x * lax.rsqrt(jnp.mean(x * x, axis=-1, keepdims=True) + EPS) * g


def _dot(a, b):
    return jnp.dot(a, b, preferred_element_type=F32)


def _dot_nt(a, b):
    return lax.dot_general(a, b, NT_DIMS, preferred_element_type=F32)


def _norm_proj_kernel(x_ref, g_ref, w_ref, *out_refs, n_groups, transposed):
    h = _rms(x_ref[0], g_ref[...]).astype(BF16)
    t_refs = dict(zip(transposed, out_refs[n_groups:]))
    off = 0
    for grp, o_ref in enumerate(out_refs[:n_groups]):
        n = o_ref.shape[-1]
        r = _dot(h, w_ref[:, off:off + n])
        o_ref[0] = r
        if grp in t_refs:
            t_refs[grp][0] = r.T
        off += n


def _norm_proj(x, g, w, widths, tm, transposed=()):
    b, s, d = x.shape
    return pl.pallas_call(
        functools.partial(_norm_proj_kernel, n_groups=len(widths), transposed=tuple(transposed)),
        grid=(b, s // tm),
        in_specs=[pl.BlockSpec((1, tm, d), lambda i, j: (i, j, 0)),
                  _resident((1, d)),
                  _resident(w.shape)],
        out_specs=([pl.BlockSpec((1, tm, n), lambda i, j: (i, j, 0)) for n in widths]
                   + [pl.BlockSpec((1, widths[t], tm), lambda i, j: (i, 0, j)) for t in transposed]),
        out_shape=([jax.ShapeDtypeStruct((b, s, n), F32) for n in widths]
                   + [jax.ShapeDtypeStruct((b, widths[t], s), F32) for t in transposed]),
        compiler_params=_params("parallel", "parallel"),
        name="norm_proj",
    )(x, g.reshape(1, d), w)


def _attn_a_kernel(slopes_ref, q_ref, k_ref, v_ref, o_ref, os_ref, ls_ref, *, seq):
    pair = pl.program_id(1)
    slope_a = slopes_ref[2 * pair]
    slope_b = slopes_ref[2 * pair + 1]
    is_a = lax.broadcasted_iota(jnp.int32, (1, 128), 1) < A_HEAD_DIM

    def block(pat, dil, q_start, k_start, nk):
        q = q_ref[0, pl.ds(q_start, BAND, stride=dil), :]
        k = k_ref[0, pl.ds(k_start, nk, stride=dil), :].astype(BF16)
        v = v_ref[0, pl.ds(k_start, nk, stride=dil), :].astype(BF16)
        q2 = jnp.concatenate([jnp.where(is_a, q, 0.0), jnp.where(is_a, 0.0, q)], axis=0).astype(BF16)
        s = _dot_nt(q2, k) * (A_HEAD_DIM ** -0.5)
        row = lax.broadcasted_iota(jnp.int32, (2 * BAND, nk), 0)
        col = lax.broadcasted_iota(jnp.int32, (2 * BAND, nk), 1)
        rel = (nk - BAND) + (row & (BAND - 1)) - col
        slope = jnp.where(row < BAND, slope_a, slope_b)
        s = s - slope * (rel * dil).astype(F32)
        s = jnp.where((rel >= 0) & (rel <= BAND), s, -jnp.inf)
        m = jnp.max(s, axis=-1, keepdims=True)
        e = jnp.exp(s - m)
        l = jnp.sum(e, axis=-1, keepdims=True)
        o = _dot(e.astype(BF16), v) / l
        lse = m + jnp.log(l)
        os_ref[pat, pl.ds(q_start, BAND, stride=dil), :] = jnp.where(is_a, o[:BAND], o[BAND:])
        ls_ref[pat, pl.ds(q_start, BAND, stride=dil), :] = jnp.where(is_a, lse[:BAND], lse[BAND:])

    for pat, (_, dil) in enumerate(A_PATTERNS):
        nb = seq // dil // BAND

        def stream(r, carry, pat=pat, dil=dil, nb=nb):
            block(pat, dil, r, r, BAND)

            def later(j, c):
                block(pat, dil, r + dil * BAND * j, r + dil * BAND * (j - 1), 2 * BAND)
                return c

            if nb > 1:
                lax.fori_loop(1, nb, later, 0)
            return carry

        lax.fori_loop(0, dil, stream, 0)

    rows = 256

    def combine(i, carry):
        sl = pl.ds(pl.multiple_of(i * rows, rows), rows)
        l0, l1, l2 = ls_ref[0, sl, :], ls_ref[1, sl, :], ls_ref[2, sl, :]
        mx = jnp.maximum(jnp.maximum(l0, l1), l2)
        w0, w1, w2 = jnp.exp(l0 - mx), jnp.exp(l1 - mx), jnp.exp(l2 - mx)
        num = w0 * os_ref[0, sl, :] + w1 * os_ref[1, sl, :] + w2 * os_ref[2, sl, :]
        o_ref[0, sl, :] = (num / (w0 + w1 + w2)).astype(o_ref.dtype)
        return carry

    lax.fori_loop(0, seq // rows, combine, 0)


def _attn_a_prompt(q, k, v, slopes):
    b, s, _ = q.shape
    assert s % (A_PATTERNS[-1][1] * BAND) == 0
    spec = pl.BlockSpec((1, s, 128), lambda i, p, *_: (i, 0, p))
    return pl.pallas_call(
        functools.partial(_attn_a_kernel, seq=s),
        grid_spec=pltpu.PrefetchScalarGridSpec(
            num_scalar_prefetch=1,
            grid=(b, A_WIDTH // 128),
            in_specs=[spec, spec, spec],
            out_specs=spec,
            scratch_shapes=[pltpu.VMEM((3, s, 128), F32), pltpu.VMEM((3, s, 128), F32)]),
        out_shape=jax.ShapeDtypeStruct((b, s, A_WIDTH), BF16),
        compiler_params=_params("parallel", "parallel"),
        name="attn_a_prompt",
    )(slopes, q, k, v)


def _pattern_multiplicity(delta):
    mult = jnp.zeros(delta.shape, F32)
    for window, dil in A_PATTERNS:
        hit = (delta >= 0) & (delta <= window)
        if dil > 1:
            hit = hit & ((delta & (dil - 1)) == 0)
        mult = mult + jnp.where(hit, 1.0, 0.0)
    return mult


def _attn_a_dec_kernel(slopes_ref, q_ref, kn_ref, vn_ref, knt_ref, vnt_ref, ck_ref, cv_ref,
                       o_ref, nk_ref, nv_ref, *, nbuf, t_new):
    half = pl.program_id(1)
    width = q_ref.shape[-1]
    heads = width // A_HEAD_DIM
    rows = heads * t_new
    q = q_ref[0]
    q_rows = jnp.concatenate([q] * heads, axis=0)
    row = lax.broadcasted_iota(jnp.int32, (rows, width), 0)
    lane = lax.broadcasted_iota(jnp.int32, (rows, width), 1)
    qm = jnp.where(row // t_new == lane // A_HEAD_DIM, q_rows, 0.0).astype(BF16)

    rcol = lax.broadcasted_iota(jnp.int32, (rows, 1), 0)
    slope = jnp.zeros((rows, 1), F32)
    for h in range(heads):
        slope = jnp.where(rcol // t_new == h, slopes_ref[half * heads + h], slope)
    tok = rcol % t_new

    s_c = _dot(qm, ck_ref[0].astype(BF16)) * (A_HEAD_DIM ** -0.5)
    pos = lax.broadcasted_iota(jnp.int32, (rows, nbuf), 1)
    delta_c = nbuf + tok - pos
    mult_c = _pattern_multiplicity(delta_c)
    s_c = jnp.where(mult_c > 0, s_c - slope * delta_c.astype(F32), -jnp.inf)
    qf = qm.astype(F32)
    knf = kn_ref[0].astype(BF16).astype(F32)
    vnf = vn_ref[0].astype(BF16).astype(F32)
    s_n, mult_n = [], []
    for t in range(t_new):
        delta = tok - t
        mult = _pattern_multiplicity(delta)
        sc = jnp.sum(qf * knf[t:t + 1, :], axis=-1, keepdims=True) * (A_HEAD_DIM ** -0.5)
        s_n.append(jnp.where(mult > 0, sc - slope * delta.astype(F32), -jnp.inf))
        mult_n.append(mult)
    m = jnp.max(s_c, axis=-1, keepdims=True)
    for sc in s_n:
        m = jnp.maximum(m, sc)
    e_c = jnp.exp(s_c - m) * mult_c
    l = jnp.sum(e_c, axis=-1, keepdims=True)
    acc = _dot_nt(e_c.astype(BF16), cv_ref[0].astype(BF16))
    for t in range(t_new):
        e = jnp.exp(s_n[t] - m) * mult_n[t]
        l = l + e
        acc = acc + e * vnf[t:t + 1, :]
    res = acc / l
    lane_head = lax.broadcasted_iota(jnp.int32, (t_new, width), 1) // A_HEAD_DIM
    out = jnp.zeros((t_new, width), F32)
    for h in range(heads):
        out = jnp.where(lane_head == h, res[h * t_new:(h + 1) * t_new], out)
    o_ref[0] = out.astype(o_ref.dtype)

    is_new = lax.broadcasted_iota(jnp.int32, (width, 128), 1) >= 128 - t_new
    for c_ref, nt_ref, n_ref in ((ck_ref, knt_ref, nk_ref), (cv_ref, vnt_ref, nv_ref)):
        rolled = pltpu.roll(c_ref[0], nbuf - t_new, 1)
        n_ref[0, :, 0:nbuf - 128] = rolled[:, 0:nbuf - 128]
        n_ref[0, :, nbuf - 128:nbuf] = jnp.where(is_new, nt_ref[0], rolled[:, nbuf - 128:nbuf])


def _attn_a_decode(q, kn, vn, cache_kt, cache_vt, slopes):
    b, t_new, _ = q.shape
    nbuf = cache_kt.shape[2]
    width = 256
    tail = lambda a: jnp.pad(a.transpose(0, 2, 1), ((0, 0), (0, 0), (128 - t_new, 0)))
    small = pl.BlockSpec((1, t_new, width), lambda i, c, *_: (i, 0, c))
    newt = pl.BlockSpec((1, width, 128), lambda i, c, *_: (i, c, 0))
    big = pl.BlockSpec((1, width, nbuf), lambda i, c, *_: (i, c, 0))
    return pl.pallas_call(
        functools.partial(_attn_a_dec_kernel, nbuf=nbuf, t_new=t_new),
        grid_spec=pltpu.PrefetchScalarGridSpec(
            num_scalar_prefetch=1,
            grid=(b, A_WIDTH // width),
            in_specs=[small, small, small, newt, newt, big, big],
            out_specs=[small, big, big]),
        out_shape=[jax.ShapeDtypeStruct((b, t_new, A_WIDTH), BF16),
                   jax.ShapeDtypeStruct(cache_kt.shape, F32),
                   jax.ShapeDtypeStruct(cache_vt.shape, F32)],
        compiler_params=_params("parallel", "parallel"),
        name="attn_a_decode",
    )(slopes, q, kn, vn, tail(kn), tail(vn), cache_kt, cache_vt)


def _retention_kernel(q_ref, k_ref, v_ref, gate_ref, cos_ref, sin_ref, dmask_ref, qdec_ref, kdec_ref,
                      cdec_ref, g_ref, s0_ref, o_ref, s_ref, *, chunk, n_chunks):
    @pl.when(pl.program_id(1) == 0)
    def _():
        s_ref[...] = s0_ref[...]

    for c in range(n_chunks):
        rows = slice(c * chunk, (c + 1) * chunk)
        cos = cos_ref[rows, :]
        sin = sin_ref[rows, :]
        for h in range(B_HEADS):
            cols = slice(h * B_DIM, (h + 1) * B_DIM)
            q = q_ref[0, rows, cols]
            k = k_ref[0, rows, cols]
            v = v_ref[0, rows, cols].astype(BF16)
            qr = (q * cos + pltpu.roll(q, B_DIM // 2, 1) * sin).astype(BF16)
            kr = (k * cos + pltpu.roll(k, B_DIM // 2, 1) * sin) * (B_DIM ** -0.5)
            att = _dot_nt(qr, kr.astype(BF16)) * dmask_ref[h]
            state = s_ref[0, h]
            o = _dot(att.astype(BF16), v) + _dot(qr, state.astype(BF16)) * qdec_ref[h]
            kd = (kr * kdec_ref[h]).astype(BF16)
            s_ref[0, h] = state * cdec_ref[h] + lax.dot_general(kd, v, TN_DIMS, preferred_element_type=F32)
            d = o - jnp.mean(o, axis=-1, keepdims=True)
            y = d * lax.rsqrt(jnp.mean(d * d, axis=-1, keepdims=True) + EPS) * g_ref[:, cols]
            o_ref[0, rows, cols] = (y * jax.nn.silu(gate_ref[0, rows, cols])).astype(o_ref.dtype)


def _retention_tables(chunk, pad):
    lg = jnp.log(1.0 - 2.0 ** (-5.0 - jnp.arange(B_HEADS, dtype=F32)))
    i = jnp.arange(pad, dtype=F32)
    live = i < chunk
    rel = i[:, None] - i[None, :]
    dmask = jnp.where((rel >= 0) & live[:, None] & live[None, :],
                      jnp.exp(jnp.maximum(rel, 0.0)[None] * lg[:, None, None]), 0.0)
    qdec = jnp.where(live[None, :], jnp.exp((i + 1.0)[None, :] * lg[:, None]), 0.0)
    kdec = jnp.where(live[None, :], jnp.exp((chunk - 1.0 - i)[None, :] * lg[:, None]), 0.0)
    cdec = jnp.exp(chunk * lg)
    bc = lambda a: jnp.broadcast_to(a[..., None], a.shape + (B_DIM,))
    return dmask, bc(qdec), bc(kdec), bc(cdec[:, None])


def _rope_tables(pos, dim, width):
    half = dim // 2
    inv = ROPE_BASE ** (-jnp.arange(half, dtype=F32) / half)
    ang = pos.astype(F32)[:, None] * inv[None, :]
    cos, sin = jnp.cos(ang), jnp.sin(ang)
    zeros = jnp.zeros((pos.shape[0], width - dim), F32)
    return (jnp.concatenate([cos, cos, zeros], axis=1), jnp.concatenate([-sin, sin, zeros], axis=1))


def _retention(q, k, v, gate, pos, g_ret, s0, chunk, rows):
    b, s, _ = q.shape
    pad = min(rows, RET_CHUNK)
    cos, sin = _rope_tables(pos, B_DIM, B_DIM)
    dmask, qdec, kdec, cdec = _retention_tables(chunk, pad)
    tok = pl.BlockSpec((1, rows, B_WIDTH), lambda i, j: (i, j, 0))
    tab = pl.BlockSpec((rows, B_DIM), lambda i, j: (j, 0))
    st = pl.BlockSpec((1, B_HEADS, B_DIM, B_DIM), lambda i, j: (i, 0, 0, 0))
    return pl.pallas_call(
        functools.partial(_retention_kernel, chunk=pad, n_chunks=rows // pad),
        grid=(b, s // rows),
        in_specs=[tok, tok, tok, tok, tab, tab, _resident(dmask.shape), _resident(qdec.shape),
                  _resident(kdec.shape), _resident(cdec.shape), _resident((1, B_WIDTH)), st],
        out_specs=[tok, st],
        out_shape=[jax.ShapeDtypeStruct((b, s, B_WIDTH), BF16),
                   jax.ShapeDtypeStruct((b, B_HEADS, B_DIM, B_DIM), F32)],
        compiler_params=_params("parallel", "arbitrary"),
        name="retention",
    )(q, k, v, gate, cos, sin, dmask, qdec, kdec, cdec, g_ret.reshape(1, B_WIDTH), s0)


def _out_proj_kernel(a_ref, b_ref, w_ref, x_ref, y_ref):
    na = a_ref.shape[-1]
    y_ref[...] = x_ref[...] + _dot(a_ref[...], w_ref[:na, :]) + _dot(b_ref[...], w_ref[na:, :])


def _out_proj(a, b, w, x, tm):
    m, d = x.shape
    return pl.pallas_call(
        _out_proj_kernel,
        grid=(m // tm,),
        in_specs=[pl.BlockSpec((tm, a.shape[1]), lambda i: (i, 0)),
                  pl.BlockSpec((tm, b.shape[1]), lambda i: (i, 0)),
                  _resident(w.shape),
                  pl.BlockSpec((tm, d), lambda i: (i, 0))],
        out_specs=pl.BlockSpec((tm, d), lambda i: (i, 0)),
        out_shape=jax.ShapeDtypeStruct((m, d), F32),
        compiler_params=_params("parallel"),
        name="out_proj",
    )(a, b, w, x)


def _rope_group(x, cos, sin):
    return x * cos + (pltpu.roll(x, C_ROPE // 2, 1) + pltpu.roll(x, 128 - C_ROPE // 2, 1)) * sin


def _mla_prep_kernel(x_ref, g_ref, wdq_ref, gq_ref, wuqn_ref, wuqr_ref, wdkv_ref, gkv_ref, wukt_ref,
                     cos_ref, sin_ref, qcat_ref, kcat_ref, ckv_ref, kr_ref):
    h = _rms(x_ref[0], g_ref[...]).astype(BF16)
    cq = _rms(_dot(h, wdq_ref[...]), gq_ref[...]).astype(BF16)
    q_nope = _dot(cq, wuqn_ref[...])
    q_rope = _dot(cq, wuqr_ref[...])
    kv = _dot(h, wdkv_ref[...])
    cos = cos_ref[...]
    sin = sin_ref[...]
    c_kv = _rms(kv[:, :C_KV_LORA], gkv_ref[...])
    k_rope = _rope_group(kv[:, C_KV_LORA:], cos, sin)
    ckv_ref[0] = c_kv
    kr_ref[0] = k_rope.T[:C_ROPE, :]
    kcat_ref[0, :, :C_KV_LORA] = c_kv.astype(BF16)
    kcat_ref[0, :, C_KV_LORA:] = k_rope.astype(BF16)
    for hd in range(C_HEADS):
        cols = slice(hd * 128, (hd + 1) * 128)
        qcat_ref[0, hd, :, :C_KV_LORA] = _dot(q_nope[:, cols].astype(BF16), wukt_ref[hd]).astype(BF16)
        qcat_ref[0, hd, :, C_KV_LORA:] = _rope_group(q_rope[:, cols], cos, sin).astype(BF16)


def _mla_prep(x, pos, g, wdq, gq, wuqn, wuqr, wdkv, gkv, wukt, tm):
    b, s, d = x.shape
    cos, sin = _rope_tables(pos, C_ROPE, 128)
    tab = pl.BlockSpec((tm, 128), lambda i, j: (j, 0))
    return pl.pallas_call(
        _mla_prep_kernel,
        grid=(b, s // tm),
        in_specs=[pl.BlockSpec((1, tm, d), lambda i, j: (i, j, 0)),
                  _resident((1, d)), _resident(wdq.shape), _resident((1, C_Q_LORA)),
                  _resident(wuqn.shape), _resident(wuqr.shape), _resident(wdkv.shape),
                  _resident((1, C_KV_LORA)), _resident(wukt.shape), tab, tab],
        out_specs=[pl.BlockSpec((1, C_HEADS, tm, C_CAT), lambda i, j: (i, 0, j, 0)),
                   pl.BlockSpec((1, tm, C_CAT), lambda i, j: (i, j, 0)),
                   pl.BlockSpec((1, tm, C_KV_LORA), lambda i, j: (i, j, 0)),
                   pl.BlockSpec((1, C_ROPE, tm), lambda i, j: (i, 0, j))],
        out_shape=[jax.ShapeDtypeStruct((b, C_HEADS, s, C_CAT), BF16),
                   jax.ShapeDtypeStruct((b, s, C_CAT), BF16),
                   jax.ShapeDtypeStruct((b, s, C_KV_LORA), F32),
                   jax.ShapeDtypeStruct((b, C_ROPE, s), F32)],
        compiler_params=_params("parallel", "parallel"),
        name="mla_prep",
    )(x, g.reshape(1, d), wdq, gq.reshape(1, -1), wuqn, wuqr, wdkv, gkv.reshape(1, -1), wukt, cos, sin)


def _mla_out(o_lat_heads, wuv_ref, wo_ref, x):
    o = jnp.concatenate([_dot(o_lat_heads[hd].astype(BF16), wuv_ref[hd]) for hd in range(C_HEADS)], axis=1)
    return x + _dot(o.astype(BF16), wo_ref[...])


def _mla_flash_kernel(q_ref, k_ref, x_ref, wuv_ref, wo_ref, y_ref, m_ref, l_ref, acc_ref, *, tq, tk):
    i = pl.program_id(1)
    j = pl.program_id(2)
    last = (i * tq + tq - 1) // tk

    @pl.when(j == 0)
    def _():
        m_ref[...] = jnp.full(m_ref.shape, -jnp.inf, F32)
        l_ref[...] = jnp.zeros(l_ref.shape, F32)
        acc_ref[...] = jnp.zeros(acc_ref.shape, F32)

    @pl.when(j <= last)
    def _():
        q = q_ref[0].reshape(C_HEADS * tq, C_CAT)
        k = k_ref[0]
        s = _dot_nt(q, k) * MLA_SCALE
        qpos = i * tq + lax.broadcasted_iota(jnp.int32, s.shape, 0) % tq
        kpos = j * tk + lax.broadcasted_iota(jnp.int32, s.shape, 1)
        s = jnp.where(kpos <= qpos, s, -jnp.inf)
        m_old = m_ref[...]
        m_new = jnp.maximum(m_old, jnp.max(s, axis=-1, keepdims=True))
        alpha = jnp.exp(m_old - m_new)
        p = jnp.exp(s - m_new)
        l_ref[...] = alpha * l_ref[...] + jnp.sum(p, axis=-1, keepdims=True)
        acc_ref[...] = alpha * acc_ref[...] + _dot(p.astype(BF16), k[:, :C_KV_LORA])
        m_ref[...] = m_new

    @pl.when(j == last)
    def _():
        o_lat = acc_ref[...] / l_ref[...]
        heads = [o_lat[hd * tq:(hd + 1) * tq] for hd in range(C_HEADS)]
        y_ref[0] = _mla_out(heads, wuv_ref, wo_ref, x_ref[0])


def _mla_flash(qcat, kcat, x, wuv, wo, tq, tk):
    b, s, d = x.shape
    return pl.pallas_call(
        functools.partial(_mla_flash_kernel, tq=tq, tk=tk),
        grid=(b, s // tq, s // tk),
        in_specs=[pl.BlockSpec((1, C_HEADS, tq, C_CAT), lambda i, j, kk: (i, 0, j, 0)),
                  pl.BlockSpec((1, tk, C_CAT),
                               lambda i, j, kk: (i, jnp.minimum(kk, (j * tq + tq - 1) // tk), 0)),
                  pl.BlockSpec((1, tq, d), lambda i, j, kk: (i, j, 0)),
                  _resident(wuv.shape), _resident(wo.shape)],
        out_specs=pl.BlockSpec((1, tq, d), lambda i, j, kk: (i, j, 0)),
        out_shape=jax.ShapeDtypeStruct((b, s, d), F32),
        scratch_shapes=[pltpu.VMEM((C_HEADS * tq, 1), F32), pltpu.VMEM((C_HEADS * tq, 1), F32),
                        pltpu.VMEM((C_HEADS * tq, C_KV_LORA), F32)],
        compiler_params=_params("parallel", "parallel", "arbitrary"),
        name="mla_flash",
    )(qcat, kcat, x, wuv, wo)


def _mla_dec_kernel(pt_ref, q_ref, kn_ref, cpool_ref, rpool_ref, o_ref, cbuf, rbuf, sem,
                    *, n_pages, t_new):
    b = pl.program_id(0)
    slot = b % 2

    def page_copies(batch, sl, pg):
        pid = pt_ref[pg, batch]
        lanes = pl.ds(pl.multiple_of(pg * PAGE, PAGE), PAGE)
        return (pltpu.make_async_copy(cpool_ref.at[pid], cbuf.at[sl, pg], sem.at[0, sl]),
                pltpu.make_async_copy(rpool_ref.at[pid], rbuf.at[sl, :, lanes], sem.at[1, sl]))

    def start_batch(batch, sl):
        def body(pg, c):
            for cp in page_copies(batch, sl, pg):
                cp.start()
            return c
        lax.fori_loop(0, n_pages, body, 0)

    @pl.when(b == 0)
    def _():
        start_batch(0, 0)

    @pl.when(b + 1 < pl.num_programs(0))
    def _():
        start_batch(b + 1, 1 - slot)

    def wait_body(pg, c):
        for cp in page_copies(b, slot, pg):
            cp.wait()
        return c
    lax.fori_loop(0, n_pages, wait_body, 0)

    q = q_ref[0]
    rows = q.shape[0]
    c_past = cbuf[slot].reshape(n_pages * PAGE, C_KV_LORA).astype(BF16)
    r_past = rbuf[slot].astype(BF16)
    s = (_dot_nt(q[:, :C_KV_LORA], c_past)
         + _dot(q[:, C_KV_LORA:C_KV_LORA + C_ROPE], r_past)) * MLA_SCALE
    qf = q.astype(F32)
    knf = kn_ref[0].astype(F32)
    tok = lax.broadcasted_iota(jnp.int32, (rows, 1), 0) % t_new
    s_n = []
    for t in range(t_new):
        sc = jnp.sum(qf * knf[t:t + 1, :], axis=-1, keepdims=True) * MLA_SCALE
        s_n.append(jnp.where(tok >= t, sc, -jnp.inf))
    m = jnp.max(s, axis=-1, keepdims=True)
    for sc in s_n:
        m = jnp.maximum(m, sc)
    p = jnp.exp(s - m)
    l = jnp.sum(p, axis=-1, keepdims=True)
    acc = _dot(p.astype(BF16), c_past)
    for t in range(t_new):
        e = jnp.exp(s_n[t] - m)
        l = l + e
        acc = acc + e * knf[t:t + 1, :C_KV_LORA]
    o_ref[0] = (acc / l).astype(o_ref.dtype)


def _mla_decode(q_rows, k_new, c_pool, r_pool_t, page_table_t):
    b, rows, _ = q_rows.shape
    t_new = k_new.shape[1]
    n_pages = page_table_t.shape[0]
    return pl.pallas_call(
        functools.partial(_mla_dec_kernel, n_pages=n_pages, t_new=t_new),
        grid_spec=pltpu.PrefetchScalarGridSpec(
            num_scalar_prefetch=1,
            grid=(b,),
            in_specs=[pl.BlockSpec((1, rows, C_CAT), lambda i, *_: (i, 0, 0)),
                      pl.BlockSpec((1, t_new, C_CAT), lambda i, *_: (i, 0, 0)),
                      pl.BlockSpec(memory_space=pl.ANY),
                      pl.BlockSpec(memory_space=pl.ANY)],
            out_specs=pl.BlockSpec((1, rows, C_KV_LORA), lambda i, *_: (i, 0, 0)),
            scratch_shapes=[pltpu.VMEM((2, n_pages, PAGE, C_KV_LORA), F32),
                            pltpu.VMEM((2, C_ROPE, n_pages * PAGE), F32),
                            pltpu.SemaphoreType.DMA((2, 2))]),
        out_shape=jax.ShapeDtypeStruct((b, rows, C_KV_LORA), BF16),
        compiler_params=_params("arbitrary"),
        name="mla_decode",
    )(page_table_t, q_rows, k_new, c_pool, r_pool_t)


def _mla_out_kernel(o_ref, wuv_ref, wo_ref, x_ref, y_ref):
    y_ref[...] = _mla_out([o_ref[hd] for hd in range(C_HEADS)], wuv_ref, wo_ref, x_ref[...])


def _mla_out_proj(o_heads, wuv, wo, x):
    m, d = x.shape
    return pl.pallas_call(
        _mla_out_kernel,
        grid=(1,),
        in_specs=[_resident(o_heads.shape), _resident(wuv.shape), _resident(wo.shape), _resident(x.shape)],
        out_specs=pl.BlockSpec((m, d), lambda i: (0, 0)),
        out_shape=jax.ShapeDtypeStruct((m, d), F32),
        compiler_params=_params("arbitrary"),
        name="mla_out_proj",
    )(o_heads, wuv, wo, x)


def _ffn_kernel(*refs, tm, tiles_per_seq, seq_in_tile, final_norm):
    x_ref, g_ref, wup_ref, wconv_ref, bconv_ref, wdown_ref = refs[:6]
    refs = refs[6:]
    if seq_in_tile:
        e1_ref, e2_ref = refs[:2]
        refs = refs[2:]
    if final_norm:
        gfin_ref = refs[0]
        refs = refs[1:]
    y_ref, gate_out_ref, gbuf = refs
    halo = 8

    if seq_in_tile:
        gbuf[0:halo, :] = jnp.zeros((halo, D_FF), F32)
    else:
        @pl.when(pl.program_id(0) % tiles_per_seq == 0)
        def _():
            gbuf[0:halo, :] = jnp.zeros((halo, D_FF), F32)

    x = x_ref[...]
    h = _rms(x, g_ref[...]).astype(BF16)
    gate = _dot(h, wup_ref[:, :D_FF])
    up = _dot(h, wup_ref[:, D_FF:])
    gbuf[halo:halo + tm, :] = gate
    tap0 = gbuf[halo - 2:halo - 2 + tm, :]
    tap1 = gbuf[halo - 1:halo - 1 + tm, :]
    if seq_in_tile:
        tok = lax.broadcasted_iota(jnp.int32, (tm, 1), 0) % seq_in_tile
        tap0 = jnp.where(tok < 2, e2_ref[...], tap0)
        tap1 = jnp.where(tok < 1, e1_ref[...], tap1)
        gate_out_ref[...] = gate
    else:
        gate_out_ref[0] = gate[tm - halo:, :]
        gbuf[0:halo, :] = gate[tm - halo:, :]
    conv = tap0 * wconv_ref[0:1, :] + bconv_ref[...] + tap1 * wconv_ref[1:2, :] + gate * wconv_ref[2:3, :]
    act = (jax.nn.silu(conv) * up).astype(BF16)
    y = x + _dot(act, wdown_ref[...])
    if final_norm:
        y = _rms(y, gfin_ref[...])
    y_ref[...] = y


def _conv_ffn(x, g, wup, wconv, bconv, wdown, *, tm, seq, conv_state=None, g_final=None):
    m, d = x.shape
    seq_in_tile = 0 if conv_state is None else seq
    row = pl.BlockSpec((tm, d), lambda i: (i, 0))
    wide = pl.BlockSpec((tm, D_FF), lambda i: (i, 0))
    args = [x, g.reshape(1, d), wup, wconv, bconv.reshape(1, D_FF), wdown]
    in_specs = [row, _resident((1, d)), _resident(wup.shape), _resident(wconv.shape),
                _resident((1, D_FF)), _resident(wdown.shape)]
    if conv_state is not None:
        assert tm % seq == 0 and seq >= 2
        zeros = jnp.zeros((m // seq, seq - 2, D_FF), F32)
        e2 = jnp.concatenate([conv_state, zeros], axis=1).reshape(m, D_FF)
        e1 = jnp.concatenate([conv_state[:, 1:], zeros, zeros[:, :1]], axis=1).reshape(m, D_FF)
        args += [e1, e2]
        in_specs += [wide, wide]
        gate_spec = wide
        gate_shape = jax.ShapeDtypeStruct((m, D_FF), F32)
    else:
        assert seq % tm == 0
        gate_spec = pl.BlockSpec((1, 8, D_FF), lambda i: (i // (seq // tm), 0, 0))
        gate_shape = jax.ShapeDtypeStruct((m // seq, 8, D_FF), F32)
    if g_final is not None:
        args.append(g_final.reshape(1, d))
        in_specs.append(_resident((1, d)))
    return pl.pallas_call(
        functools.partial(_ffn_kernel, tm=tm, tiles_per_seq=max(seq // tm, 1), seq_in_tile=seq_in_tile,
                          final_norm=g_final is not None),
        grid=(m // tm,),
        in_specs=in_specs,
        out_specs=[row, gate_spec],
        out_shape=[jax.ShapeDtypeStruct((m, d), F32), gate_shape],
        scratch_shapes=[pltpu.VMEM((tm + 8, D_FF), F32)],
        compiler_params=_params("arbitrary"),
        name="conv_ffn",
    )(*args)


def _prepare_weights(w_in_ab, w_out_ab, w_dq, w_uq, w_dkv, w_uk, w_uv, w_o_c, w_up, w_down):
    per_head = C_NOPE + C_ROPE
    uq = w_uq[0].reshape(C_Q_LORA, C_HEADS, per_head)
    wuqr = jnp.pad(uq[:, :, C_NOPE:], ((0, 0), (0, 0), (0, 128 - C_ROPE)))
    return dict(
        w_in=w_in_ab[0].astype(BF16),
        w_out=w_out_ab[0].astype(BF16),
        wdq=w_dq[0].astype(BF16),
        wuqn=uq[:, :, :C_NOPE].reshape(C_Q_LORA, C_HEADS * C_NOPE).astype(BF16),
        wuqr=wuqr.reshape(C_Q_LORA, C_HEADS * 128).astype(BF16),
        wdkv=jnp.pad(w_dkv[0], ((0, 0), (0, 128 - C_ROPE))).astype(BF16),
        wukt=w_uk[0].transpose(1, 2, 0).astype(BF16),
        wuv=w_uv[0].transpose(1, 0, 2).astype(BF16),
        wo=w_o_c[0].astype(BF16),
        w_up=w_up.astype(BF16),
        w_down=w_down.astype(BF16),
    )


def _alibi_slopes():
    return 2.0 ** (-8.0 * (jnp.arange(A_HEADS, dtype=F32) + 1.0) / A_HEADS)


AB_WIDTHS = (A_WIDTH,) * 3 + (B_WIDTH,) * 4


def _prompt_trunk(x, w, slopes, g_mix, g_ffn, g_final, g_ret, g_q, g_kv, w_conv, b_conv):
    b, s, d = x.shape
    m = b * s
    pos = jnp.arange(s)
    qa, ka, va, qb, kb, vb, gb, ka_t, va_t = _norm_proj(x, g_mix[0], w["w_in"], AB_WIDTHS, tm=512,
                                                        transposed=(1, 2))
    x = x.reshape(m, d)
    o_a = _attn_a_prompt(qa, ka, va, slopes)
    o_b, ret = _retention(qb, kb, vb, gb, pos, g_ret[0], jnp.zeros((b, B_HEADS, B_DIM, B_DIM), F32),
                          RET_CHUNK, rows=512)
    x = _out_proj(o_a.reshape(m, -1), o_b.reshape(m, -1), w["w_out"], x, tm=min(1024, m))
    tm_ffn = 256
    x, tails0 = _conv_ffn(x, g_ffn[0], w["w_up"][0], w_conv[0], b_conv[0], w["w_down"][0], tm=tm_ffn, seq=s)
    qcat, kcat, c_kv, k_rope_t = _mla_prep(x.reshape(b, s, d), pos, g_mix[1], w["wdq"], g_q[0], w["wuqn"],
                                           w["wuqr"], w["wdkv"], g_kv[0], w["wukt"], tm=256)
    x = _mla_flash(qcat, kcat, x.reshape(b, s, d), w["wuv"], w["wo"], tq=128, tk=512).reshape(m, d)
    y, tails1 = _conv_ffn(x, g_ffn[1], w["w_up"][1], w_conv[1], b_conv[1], w["w_down"][1], tm=tm_ffn, seq=s,
                          g_final=g_final)
    new_conv = jnp.stack([tails0[:, 6:8], tails1[:, 6:8]])
    keep = min(A_PATTERNS[-1][0], s)
    heads = lambda a: a[:, :, s - keep:].reshape(b, A_HEADS, A_HEAD_DIM, keep).transpose(0, 3, 1, 2)[None]
    return (y.reshape(b, s, d), heads(ka_t), heads(va_t), ret[None], c_kv[None],
            k_rope_t.transpose(0, 2, 1)[None], new_conv)


def _sample_trunk(x, w, slopes, g_mix, g_ffn, g_final, g_ret, g_q, g_kv, w_conv, b_conv,
                  cache_k, cache_v, state_ret, conv_state, c_pool, r_pool, page_table):
    b, t, d = x.shape
    m = b * t
    pos = PAST_LEN + jnp.arange(t)
    x = x.reshape(m, d)
    qa, ka, va, qb, kb, vb, gb = [a.reshape(b, t, -1) for a in
                                  _norm_proj(x.reshape(1, m, d), g_mix[0], w["w_in"], AB_WIDTHS, tm=m)]
    nbuf = cache_k.shape[1]
    feature_major = lambda a: a.transpose(0, 2, 3, 1).reshape(b, A_WIDTH, nbuf)
    o_a, new_k, new_v = _attn_a_decode(qa, ka, va, feature_major(cache_k), feature_major(cache_v), slopes)
    padr = lambda a: jnp.pad(a, ((0, 0), (0, RET_CHUNK - t), (0, 0)))
    o_b, ret = _retention(padr(qb), padr(kb), padr(vb), padr(gb), jnp.pad(pos, (0, RET_CHUNK - t)),
                          g_ret[0], state_ret, t, rows=RET_CHUNK)
    x = _out_proj(o_a.reshape(m, -1), o_b[:, :t].reshape(m, -1), w["w_out"], x, tm=m)
    x, gate0 = _conv_ffn(x, g_ffn[0], w["w_up"][0], w_conv[0], b_conv[0], w["w_down"][0], tm=m, seq=t,
                         conv_state=conv_state[0])
    qcat, kcat, c_kv, k_rope_t = _mla_prep(x.reshape(1, m, d), jnp.tile(pos, b), g_mix[1], w["wdq"], g_q[0],
                                           w["wuqn"], w["wuqr"], w["wdkv"], g_kv[0], w["wukt"], tm=m)
    q_rows = qcat.reshape(C_HEADS, b, t, C_CAT).transpose(1, 0, 2, 3).reshape(b, C_HEADS * t, C_CAT)
    o_lat = _mla_decode(q_rows, kcat.reshape(b, t, C_CAT), c_pool, r_pool.transpose(0, 2, 1), page_table.T)
    o_heads = o_lat.reshape(b, C_HEADS, t, C_KV_LORA).transpose(1, 0, 2, 3).reshape(C_HEADS, m, C_KV_LORA)
    x = _mla_out_proj(o_heads, w["wuv"], w["wo"], x)
    y, gate1 = _conv_ffn(x, g_ffn[1], w["w_up"][1], w_conv[1], b_conv[1], w["w_down"][1], tm=m, seq=t,
                         conv_state=conv_state[1], g_final=g_final)
    new_conv = jnp.stack([gate0.reshape(b, t, D_FF)[:, t - 2:], gate1.reshape(b, t, D_FF)[:, t - 2:]])
    heads = lambda a: a.reshape(b, A_HEADS, A_HEAD_DIM, nbuf).transpose(0, 3, 1, 2)[None]
    return (y.reshape(b, t, d), heads(new_k), heads(new_v), ret[None], c_kv.reshape(1, b, t, C_KV_LORA),
            k_rope_t.reshape(C_ROPE, b, t).transpose(1, 2, 0)[None], new_conv)


def kernel(x_prompt, x_sample, cache_a_k, cache_a_v, state_ret, cache_c_kv, cache_k_rope, state_ffn_conv,
           page_table, g_mix, g_ffn, g_final, w_in_ab, w_out_ab, g_ret, w_dq, g_q, w_uq, w_dkv, g_kv,
           w_uk, w_uv, w_o_c, w_up, w_conv, b_conv, w_down):
    assert g_mix.shape[0] == 2 and w_in_ab.shape[0] == 1 and w_dq.shape[0] == 1
    w = _prepare_weights(w_in_ab, w_out_ab, w_dq, w_uq, w_dkv, w_uk, w_uv, w_o_c, w_up, w_down)
    shared = (w, _alibi_slopes(), g_mix, g_ffn, g_final, g_ret, g_q, g_kv, w_conv, b_conv)
    out_p = _prompt_trunk(x_prompt, *shared)
    out_s = _sample_trunk(x_sample, *shared, cache_a_k[0], cache_a_v[0], state_ret[0], state_ffn_conv,
                          cache_c_kv[0], cache_k_rope[0], page_table)
    return (out_p[0], out_s[0]) + out_p[1:] + out_s[1:]
```

```python
import functools

import jax
import jax.numpy as jnp
from jax import lax
from jax.experimental import pallas as pl
from jax.experimental.pallas import tpu as pltpu

F32 = jnp.float32
BF16 = jnp.bfloat16

D_MODEL = 1024
A_HEADS = 8
A_HEAD_DIM = 64
A_WIDTH = A_HEADS * A_HEAD_DIM
A_PATTERNS = ((128, 1), (512, 4), (2048, 16))
BAND = 128
B_HEADS = 4
B_DIM = 128
B_WIDTH = B_HEADS * B_DIM
RET_CHUNK = 128
C_HEADS = 8
C_Q_LORA = 384
C_KV_LORA = 256
C_NOPE = 128
C_ROPE = 64
C_V_DIM = 128
C_CAT = C_KV_LORA + 128
MLA_SCALE = (C_NOPE + C_ROPE) ** -0.5
D_FF = 2816
ROPE_BASE = 10000.0
EPS = 1e-6
PAST_LEN = 8192
PAGE = 128

VMEM_LIMIT = 56 * 1024 * 1024

NT_DIMS = (((1,), (1,)), ((), ()))
TN_DIMS = (((0,), (0,)), ((), ()))


def _params(*sem):
    return pltpu.CompilerParams(dimension_semantics=sem, vmem_limit_bytes=VMEM_LIMIT)


def _resident(shape):
    nd = len(shape)
    return pl.BlockSpec(shape, lambda *_: (0,) * nd, pipeline_mode=pl.Buffered(1))


def _rms(x, g):
    return x * lax.rsqrt(jnp.mean(x * x, axis=-1, keepdims=True) + EPS) * g


def _dot(a, b):
    return jnp.dot(a, b, preferred_element_type=F32)


def _dot_nt(a, b):
    return lax.dot_general(a, b, NT_DIMS, preferred_element_type=F32)


def _norm_proj_kernel(x_ref, g_ref, w_ref, *out_refs, n_groups, transposed):
    h = _rms(x_ref[0], g_ref[...]).astype(BF16)
    t_refs = dict(zip(transposed, out_refs[n_groups:]))
    off = 0
    for grp, o_ref in enumerate(out_refs[:n_groups]):
        n = o_ref.shape[-1]
        r = _dot(h, w_ref[:, off:off + n])
        o_ref[0] = r
        if grp in t_refs:
            t_refs[grp][0] = r.T
        off += n


def _norm_proj(x, g, w, widths, tm, transposed=()):
    b, s, d = x.shape
    return pl.pallas_call(
        functools.partial(_norm_proj_kernel, n_groups=len(widths), transposed=tuple(transposed)),
        grid=(b, s // tm),
        in_specs=[pl.BlockSpec((1, tm, d), lambda i, j: (i, j, 0)),
                  _resident((1, d)),
                  _resident(w.shape)],
        out_specs=([pl.BlockSpec((1, tm, n), lambda i, j: (i, j, 0)) for n in widths]
                   + [pl.BlockSpec((1, widths[t], tm), lambda i, j: (i, 0, j)) for t in transposed]),
        out_shape=([jax.ShapeDtypeStruct((b, s, n), F32) for n in widths]
                   + [jax.ShapeDtypeStruct((b, widths[t], s), F32) for t in transposed]),
        compiler_params=_params("parallel", "parallel"),
        name="norm_proj",
    )(x, g.reshape(1, d), w)


def _unrolled_loop(n_items, fn):
    unroll = next(u for u in (4, 3, 2, 1) if n_items % u == 0)
    if n_items <= unroll:
        for t in range(n_items):
            fn(t)
        return

    def body(it, carry):
        for u in range(unroll):
            fn(it * unroll + u)
        return carry

    lax.fori_loop(0, n_items // unroll, body, 0)


def _attn_a_kernel(slopes_ref, q_ref, k_ref, v_ref, o_ref, os_ref, ls_ref, bias_ref, *, seq):
    pair = pl.program_id(1)
    is_a = lax.broadcasted_iota(jnp.int32, (1, 128), 1) < A_HEAD_DIM

    row = lax.broadcasted_iota(jnp.int32, (2 * BAND, 2 * BAND), 0)
    col = lax.broadcasted_iota(jnp.int32, (2 * BAND, 2 * BAND), 1)
    rel = BAND + (row & (BAND - 1)) - col
    slope = jnp.where(row < BAND, slopes_ref[2 * pair], slopes_ref[2 * pair + 1])
    in_band = (rel >= 0) & (rel <= BAND)
    for pat, (_, dil) in enumerate(A_PATTERNS):
        bias_ref[pat] = jnp.where(in_band, -slope * (rel * dil).astype(F32), -jnp.inf)

    def block(pat, dil, q_start, k_start, first):
        nk = BAND if first else 2 * BAND
        q = q_ref[0, pl.ds(q_start, BAND, stride=dil), :] * (A_HEAD_DIM ** -0.5)
        k = k_ref[0, pl.ds(k_start, nk, stride=dil), :].astype(BF16)
        v = v_ref[0, pl.ds(k_start, nk, stride=dil), :].astype(BF16)
        q2 = jnp.concatenate([jnp.where(is_a, q, 0.0), jnp.where(is_a, 0.0, q)], axis=0).astype(BF16)
        s = _dot_nt(q2, k) + (bias_ref[pat, :, BAND:] if first else bias_ref[pat])
        m = jnp.max(s, axis=-1, keepdims=True)
        e = jnp.exp(s - m)
        l = jnp.sum(e, axis=-1, keepdims=True)
        o = _dot(e.astype(BF16), v) / l
        lse = m + jnp.log(l)
        os_ref[pat, pl.ds(q_start, BAND, stride=dil), :] = jnp.where(is_a, o[:BAND], o[BAND:])
        ls_ref[pat, pl.ds(q_start, BAND, stride=dil), :] = jnp.where(is_a, lse[:BAND], lse[BAND:])

    for pat, (_, dil) in enumerate(A_PATTERNS):
        nb = seq // dil // BAND

        def first_block(r, pat=pat, dil=dil):
            block(pat, dil, r, r, True)

        def later_block(t, pat=pat, dil=dil, nb=nb):
            r = t // (nb - 1)
            j = 1 + t % (nb - 1)
            block(pat, dil, r + dil * BAND * j, r + dil * BAND * (j - 1), False)

        _unrolled_loop(dil, first_block)
        if nb > 1:
            _unrolled_loop(dil * (nb - 1), later_block)

    rows = 256

    def combine(i, carry):
        sl = pl.ds(pl.multiple_of(i * rows, rows), rows)
        l0, l1, l2 = ls_ref[0, sl, :], ls_ref[1, sl, :], ls_ref[2, sl, :]
        mx = jnp.maximum(jnp.maximum(l0, l1), l2)
        w0, w1, w2 = jnp.exp(l0 - mx), jnp.exp(l1 - mx), jnp.exp(l2 - mx)
        num = w0 * os_ref[0, sl, :] + w1 * os_ref[1, sl, :] + w2 * os_ref[2, sl, :]
        o_ref[0, sl, :] = (num / (w0 + w1 + w2)).astype(o_ref.dtype)
        return carry

    lax.fori_loop(0, seq // rows, combine, 0)


def _attn_a_prompt(q, k, v, slopes):
    b, s, _ = q.shape
    assert s % (A_PATTERNS[-1][1] * BAND) == 0
    spec = pl.BlockSpec((1, s, 128), lambda i, p, *_: (i, 0, p))
    return pl.pallas_call(
        functools.partial(_attn_a_kernel, seq=s),
        grid_spec=pltpu.PrefetchScalarGridSpec(
            num_scalar_prefetch=1,
            grid=(b, A_WIDTH // 128),
            in_specs=[spec, spec, spec],
            out_specs=spec,
            scratch_shapes=[pltpu.VMEM((3, s, 128), F32), pltpu.VMEM((3, s, 128), F32),
                            pltpu.VMEM((3, 2 * BAND, 2 * BAND), F32)]),
        out_shape=jax.ShapeDtypeStruct((b, s, A_WIDTH), BF16),
        compiler_params=_params("parallel", "parallel"),
        name="attn_a_prompt",
    )(slopes, q, k, v)


def _pattern_multiplicity(delta):
    mult = jnp.zeros(delta.shape, F32)
    for window, dil in A_PATTERNS:
        hit = (delta >= 0) & (delta <= window)
        if dil > 1:
            hit = hit & ((delta & (dil - 1)) == 0)
        mult = mult + jnp.where(hit, 1.0, 0.0)
    return mult


def _attn_a_dec_kernel(slopes_ref, q_ref, kn_ref, vn_ref, knt_ref, vnt_ref, ck_ref, cv_ref,
                       o_ref, nk_ref, nv_ref, *, nbuf, t_new):
    half = pl.program_id(1)
    width = q_ref.shape[-1]
    heads = width // A_HEAD_DIM
    rows = heads * t_new
    q = q_ref[0]
    q_rows = jnp.concatenate([q] * heads, axis=0)
    row = lax.broadcasted_iota(jnp.int32, (rows, width), 0)
    lane = lax.broadcasted_iota(jnp.int32, (rows, width), 1)
    qm = jnp.where(row // t_new == lane // A_HEAD_DIM, q_rows, 0.0).astype(BF16)

    rcol = lax.broadcasted_iota(jnp.int32, (rows, 1), 0)
    slope = jnp.zeros((rows, 1), F32)
    for h in range(heads):
        slope = jnp.where(rcol // t_new == h, slopes_ref[half * heads + h], slope)
    tok = rcol % t_new

    s_c = _dot(qm, ck_ref[0].astype(BF16)) * (A_HEAD_DIM ** -0.5)
    pos = lax.broadcasted_iota(jnp.int32, (rows, nbuf), 1)
    delta_c = nbuf + tok - pos
    mult_c = _pattern_multiplicity(delta_c)
    s_c = jnp.where(mult_c > 0, s_c - slope * delta_c.astype(F32), -jnp.inf)
    qf = qm.astype(F32)
    knf = kn_ref[0].astype(BF16).astype(F32)
    vnf = vn_ref[0].astype(BF16).astype(F32)
    s_n, mult_n = [], []
    for t in range(t_new):
        delta = tok - t
        mult = _pattern_multiplicity(delta)
        sc = jnp.sum(qf * knf[t:t + 1, :], axis=-1, keepdims=True) * (A_HEAD_DIM ** -0.5)
        s_n.append(jnp.where(mult > 0, sc - slope * delta.astype(F32), -jnp.inf))
        mult_n.append(mult)
    m = jnp.max(s_c, axis=-1, keepdims=True)
    for sc in s_n:
        m = jnp.maximum(m, sc)
    e_c = jnp.exp(s_c - m) * mult_c
    l = jnp.sum(e_c, axis=-1, keepdims=True)
    acc = _dot_nt(e_c.astype(BF16), cv_ref[0].astype(BF16))
    for t in range(t_new):
        e = jnp.exp(s_n[t] - m) * mult_n[t]
        l = l + e
        acc = acc + e * vnf[t:t + 1, :]
    res = acc / l
    lane_head = lax.broadcasted_iota(jnp.int32, (t_new, width), 1) // A_HEAD_DIM
    out = jnp.zeros((t_new, width), F32)
    for h in range(heads):
        out = jnp.where(lane_head == h, res[h * t_new:(h + 1) * t_new], out)
    o_ref[0] = out.astype(o_ref.dtype)

    is_new = lax.broadcasted_iota(jnp.int32, (width, 128), 1) >= 128 - t_new
    for c_ref, nt_ref, n_ref in ((ck_ref, knt_ref, nk_ref), (cv_ref, vnt_ref, nv_ref)):
        rolled = pltpu.roll(c_ref[0], nbuf - t_new, 1)
        n_ref[0, :, 0:nbuf - 128] = rolled[:, 0:nbuf - 128]
        n_ref[0, :, nbuf - 128:nbuf] = jnp.where(is_new, nt_ref[0], rolled[:, nbuf - 128:nbuf])


def _attn_a_decode(q, kn, vn, cache_kt, cache_vt, slopes):
    b, t_new, _ = q.shape
    nbuf = cache_kt.shape[2]
    width = 256
    tail = lambda a: jnp.pad(a.transpose(0, 2, 1), ((0, 0), (0, 0), (128 - t_new, 0)))
    small = pl.BlockSpec((1, t_new, width), lambda i, c, *_: (i, 0, c))
    newt = pl.BlockSpec((1, width, 128), lambda i, c, *_: (i, c, 0))
    big = pl.BlockSpec((1, width, nbuf), lambda i, c, *_: (i, c, 0))
    return pl.pallas_call(
        functools.partial(_attn_a_dec_kernel, nbuf=nbuf, t_new=t_new),
        grid_spec=pltpu.PrefetchScalarGridSpec(
            num_scalar_prefetch=1,
            grid=(b, A_WIDTH // width),
            in_specs=[small, small, small, newt, newt, big, big],
            out_specs=[small, big, big]),
        out_shape=[jax.ShapeDtypeStruct((b, t_new, A_WIDTH), BF16),
                   jax.ShapeDtypeStruct(cache_kt.shape, F32),
                   jax.ShapeDtypeStruct(cache_vt.shape, F32)],
        compiler_params=_params("parallel", "parallel"),
        name="attn_a_decode",
    )(slopes, q, kn, vn, tail(kn), tail(vn), cache_kt, cache_vt)


def _retention_kernel(q_ref, k_ref, v_ref, gate_ref, cos_ref, sin_ref, dmask_ref, qdec_ref, kdec_ref,
                      cdec_ref, g_ref, s0_ref, o_ref, s_ref, *, chunk, n_chunks):
    @pl.when(pl.program_id(1) == 0)
    def _():
        s_ref[...] = s0_ref[...]

    for c in range(n_chunks):
        rows = slice(c * chunk, (c + 1) * chunk)
        cos = cos_ref[rows, :]
        sin = sin_ref[rows, :]
        for h in range(B_HEADS):
            cols = slice(h * B_DIM, (h + 1) * B_DIM)
            q = q_ref[0, rows, cols]
            k = k_ref[0, rows, cols]
            v = v_ref[0, rows, cols].astype(BF16)
            qr = (q * cos + pltpu.roll(q, B_DIM // 2, 1) * sin).astype(BF16)
            kr = (k * cos + pltpu.roll(k, B_DIM // 2, 1) * sin) * (B_DIM ** -0.5)
            att = _dot_nt(qr, kr.astype(BF16)) * dmask_ref[h]
            state = s_ref[0, h]
            o = _dot(att.astype(BF16), v) + _dot(qr, state.astype(BF16)) * qdec_ref[h]
            kd = (kr * kdec_ref[h]).astype(BF16)
            s_ref[0, h] = state * cdec_ref[h] + lax.dot_general(kd, v, TN_DIMS, preferred_element_type=F32)
            d = o - jnp.mean(o, axis=-1, keepdims=True)
            y = d * lax.rsqrt(jnp.mean(d * d, axis=-1, keepdims=True) + EPS) * g_ref[:, cols]
            o_ref[0, rows, cols] = (y * jax.nn.silu(gate_ref[0, rows, cols])).astype(o_ref.dtype)


def _retention_tables(chunk, pad):
    lg = jnp.log(1.0 - 2.0 ** (-5.0 - jnp.arange(B_HEADS, dtype=F32)))
    i = jnp.arange(pad, dtype=F32)
    live = i < chunk
    rel = i[:, None] - i[None, :]
    dmask = jnp.where((rel >= 0) & live[:, None] & live[None, :],
                      jnp.exp(jnp.maximum(rel, 0.0)[None] * lg[:, None, None]), 0.0)
    qdec = jnp.where(live[None, :], jnp.exp((i + 1.0)[None, :] * lg[:, None]), 0.0)
    kdec = jnp.where(live[None, :], jnp.exp((chunk - 1.0 - i)[None, :] * lg[:, None]), 0.0)
    cdec = jnp.exp(chunk * lg)
    bc = lambda a: jnp.broadcast_to(a[..., None], a.shape + (B_DIM,))
    return dmask, bc(qdec), bc(kdec), bc(cdec[:, None])


def _rope_tables(pos, dim, width):
    half = dim // 2
    inv = ROPE_BASE ** (-jnp.arange(half, dtype=F32) / half)
    ang = pos.astype(F32)[:, None] * inv[None, :]
    cos, sin = jnp.cos(ang), jnp.sin(ang)
    zeros = jnp.zeros((pos.shape[0], width - dim), F32)
    return (jnp.concatenate([cos, cos, zeros], axis=1), jnp.concatenate([-sin, sin, zeros], axis=1))


def _retention(q, k, v, gate, pos, g_ret, s0, chunk, rows):
    b, s, _ = q.shape
    pad = min(rows, RET_CHUNK)
    cos, sin = _rope_tables(pos, B_DIM, B_DIM)
    dmask, qdec, kdec, cdec = _retention_tables(chunk, pad)
    tok = pl.BlockSpec((1, rows, B_WIDTH), lambda i, j: (i, j, 0))
    tab = pl.BlockSpec((rows, B_DIM), lambda i, j: (j, 0))
    st = pl.BlockSpec((1, B_HEADS, B_DIM, B_DIM), lambda i, j: (i, 0, 0, 0))
    return pl.pallas_call(
        functools.partial(_retention_kernel, chunk=pad, n_chunks=rows // pad),
        grid=(b, s // rows),
        in_specs=[tok, tok, tok, tok, tab, tab, _resident(dmask.shape), _resident(qdec.shape),
                  _resident(kdec.shape), _resident(cdec.shape), _resident((1, B_WIDTH)), st],
        out_specs=[tok, st],
        out_shape=[jax.ShapeDtypeStruct((b, s, B_WIDTH), BF16),
                   jax.ShapeDtypeStruct((b, B_HEADS, B_DIM, B_DIM), F32)],
        compiler_params=_params("parallel", "arbitrary"),
        name="retention",
    )(q, k, v, gate, cos, sin, dmask, qdec, kdec, cdec, g_ret.reshape(1, B_WIDTH), s0)


def _out_proj_kernel(a_ref, b_ref, w_ref, x_ref, y_ref):
    na = a_ref.shape[-1]
    y_ref[...] = x_ref[...] + _dot(a_ref[...], w_ref[:na, :]) + _dot(b_ref[...], w_ref[na:, :])


def _out_proj(a, b, w, x, tm):
    m, d = x.shape
    return pl.pallas_call(
        _out_proj_kernel,
        grid=(m // tm,),
        in_specs=[pl.BlockSpec((tm, a.shape[1]), lambda i: (i, 0)),
                  pl.BlockSpec((tm, b.shape[1]), lambda i: (i, 0)),
                  _resident(w.shape),
                  pl.BlockSpec((tm, d), lambda i: (i, 0))],
        out_specs=pl.BlockSpec((tm, d), lambda i: (i, 0)),
        out_shape=jax.ShapeDtypeStruct((m, d), F32),
        compiler_params=_params("parallel"),
        name="out_proj",
    )(a, b, w, x)


def _rope_group(x, cos, sin):
    return x * cos + (pltpu.roll(x, C_ROPE // 2, 1) + pltpu.roll(x, 128 - C_ROPE // 2, 1)) * sin


def _mla_prep_kernel(x_ref, g_ref, wdq_ref, gq_ref, wuqn_ref, wuqr_ref, wdkv_ref, gkv_ref, wukt_ref,
                     cos_ref, sin_ref, qcat_ref, kcat_ref, ckv_ref, kr_ref):
    h = _rms(x_ref[0], g_ref[...]).astype(BF16)
    cq = _rms(_dot(h, wdq_ref[...]), gq_ref[...]).astype(BF16)
    q_nope = _dot(cq, wuqn_ref[...])
    q_rope = _dot(cq, wuqr_ref[...])
    kv = _dot(h, wdkv_ref[...])
    cos = cos_ref[...]
    sin = sin_ref[...]
    c_kv = _rms(kv[:, :C_KV_LORA], gkv_ref[...])
    k_rope = _rope_group(kv[:, C_KV_LORA:], cos, sin)
    ckv_ref[0] = c_kv
    kr_ref[0] = k_rope.T[:C_ROPE, :]
    kcat_ref[0, :, :C_KV_LORA] = c_kv.astype(BF16)
    kcat_ref[0, :, C_KV_LORA:] = k_rope.astype(BF16)
    for hd in range(C_HEADS):
        cols = slice(hd * 128, (hd + 1) * 128)
        qcat_ref[0, hd, :, :C_KV_LORA] = _dot(q_nope[:, cols].astype(BF16), wukt_ref[hd]).astype(BF16)
        qcat_ref[0, hd, :, C_KV_LORA:] = _rope_group(q_rope[:, cols], cos, sin).astype(BF16)


def _mla_prep(x, pos, g, wdq, gq, wuqn, wuqr, wdkv, gkv, wukt, tm):
    b, s, d = x.shape
    cos, sin = _rope_tables(pos, C_ROPE, 128)
    tab = pl.BlockSpec((tm, 128), lambda i, j: (j, 0))
    return pl.pallas_call(
        _mla_prep_kernel,
        grid=(b, s // tm),
        in_specs=[pl.BlockSpec((1, tm, d), lambda i, j: (i, j, 0)),
                  _resident((1, d)), _resident(wdq.shape), _resident((1, C_Q_LORA)),
                  _resident(wuqn.shape), _resident(wuqr.shape), _resident(wdkv.shape),
                  _resident((1, C_KV_LORA)), _resident(wukt.shape), tab, tab],
        out_specs=[pl.BlockSpec((1, C_HEADS, tm, C_CAT), lambda i, j: (i, 0, j, 0)),
                   pl.BlockSpec((1, tm, C_CAT), lambda i, j: (i, j, 0)),
                   pl.BlockSpec((1, tm, C_KV_LORA), lambda i, j: (i, j, 0)),
                   pl.BlockSpec((1, C_ROPE, tm), lambda i, j: (i, 0, j))],
        out_shape=[jax.ShapeDtypeStruct((b, C_HEADS, s, C_CAT), BF16),
                   jax.ShapeDtypeStruct((b, s, C_CAT), BF16),
                   jax.ShapeDtypeStruct((b, s, C_KV_LORA), F32),
                   jax.ShapeDtypeStruct((b, C_ROPE, s), F32)],
        compiler_params=_params("parallel", "parallel"),
        name="mla_prep",
    )(x, g.reshape(1, d), wdq, gq.reshape(1, -1), wuqn, wuqr, wdkv, gkv.reshape(1, -1), wukt, cos, sin)


def _mla_out(o_lat_heads, wuv_ref, wo_ref, x):
    o = jnp.concatenate([_dot(o_lat_heads[hd].astype(BF16), wuv_ref[hd]) for hd in range(C_HEADS)], axis=1)
    return x + _dot(o.astype(BF16), wo_ref[...])


def _mla_flash_kernel(q_ref, k_ref, x_ref, wuv_ref, wo_ref, y_ref, m_ref, l_ref, acc_ref, *, tq, tk):
    i = pl.program_id(1)
    last = (i * tq + tq - 1) // tk
    c_exp = MLA_SCALE * 1.4426950408889634
    n_chunks = tk // 128
    rows = 512
    group = rows // tq

    m_ref[...] = jnp.full(m_ref.shape, -jnp.inf, F32)
    l_ref[...] = jnp.zeros(l_ref.shape, F32)
    acc_ref[...] = jnp.zeros(acc_ref.shape, F32)

    def key_block(j, masked):
        k = k_ref[0, pl.ds(pl.multiple_of(j * tk, tk), tk), :]
        if masked:
            qpos = i * tq + lax.broadcasted_iota(jnp.int32, (rows, 128), 0) % tq
            lane = lax.broadcasted_iota(jnp.int32, (rows, 128), 1)
        for g in range(C_HEADS // group):
            hs = slice(g * group, (g + 1) * group)
            s = _dot_nt(q_ref[0, hs].reshape(rows, C_CAT), k)
            chunks = [s[:, c * 128:(c + 1) * 128] for c in range(n_chunks)]
            if masked:
                chunks = [jnp.where(j * tk + c * 128 + lane <= qpos, sc, -jnp.inf)
                          for c, sc in enumerate(chunks)]
            part = functools.reduce(jnp.maximum, chunks)
            m_old = m_ref[hs].reshape(rows, 128)
            m_new = jnp.maximum(m_old, jnp.max(part, axis=-1, keepdims=True))
            alpha = jnp.exp2((m_old - m_new) * c_exp)
            p = [jnp.exp2((sc - m_new) * c_exp) for sc in chunks]
            row_sum = jnp.sum(functools.reduce(jnp.add, p), axis=-1, keepdims=True)
            l_ref[hs] = (alpha * l_ref[hs].reshape(rows, 128) + row_sum).reshape(group, tq, 128)
            m_ref[hs] = m_new.reshape(group, tq, 128)
            pv = _dot(jnp.concatenate(p, axis=1).astype(BF16), k[:, :C_KV_LORA])
            acc = jnp.concatenate([alpha] * (C_KV_LORA // 128), axis=1) * acc_ref[hs].reshape(rows, C_KV_LORA)
            acc_ref[hs] = (acc + pv).reshape(group, tq, C_KV_LORA)

    def body(j, carry):
        key_block(j, masked=False)
        return carry

    lax.fori_loop(0, last, body, 0)
    key_block(last, masked=True)

    heads = []
    for hd in range(C_HEADS):
        inv = 1.0 / l_ref[hd]
        heads.append(acc_ref[hd] * jnp.concatenate([inv] * (C_KV_LORA // 128), axis=1))
    y_ref[0] = _mla_out(heads, wuv_ref, wo_ref, x_ref[0])


def _mla_flash(qcat, kcat, x, wuv, wo, tq, tk):
    b, s, d = x.shape
    return pl.pallas_call(
        functools.partial(_mla_flash_kernel, tq=tq, tk=tk),
        grid=(b, s // tq),
        in_specs=[pl.BlockSpec((1, C_HEADS, tq, C_CAT), lambda i, j: (i, 0, j, 0)),
                  pl.BlockSpec((1, s, C_CAT), lambda i, j: (i, 0, 0)),
                  pl.BlockSpec((1, tq, d), lambda i, j: (i, j, 0)),
                  _resident(wuv.shape), _resident(wo.shape)],
        out_specs=pl.BlockSpec((1, tq, d), lambda i, j: (i, j, 0)),
        out_shape=jax.ShapeDtypeStruct((b, s, d), F32),
        scratch_shapes=[pltpu.VMEM((C_HEADS, tq, 128), F32), pltpu.VMEM((C_HEADS, tq, 128), F32),
                        pltpu.VMEM((C_HEADS, tq, C_KV_LORA), F32)],
        compiler_params=_params("parallel", "parallel"),
        name="mla_flash",
    )(qcat, kcat, x, wuv, wo)


def _mla_dec_kernel(pt_ref, q_ref, kn_ref, cpool_ref, rpool_ref, o_ref, cbuf, rbuf, sem,
                    *, n_pages, t_new):
    b = pl.program_id(0)
    slot = b % 2

    def page_copies(batch, sl, pg):
        pid = pt_ref[pg, batch]
        lanes = pl.ds(pl.multiple_of(pg * PAGE, PAGE), PAGE)
        return (pltpu.make_async_copy(cpool_ref.at[pid], cbuf.at[sl, pg], sem.at[0, sl]),
                pltpu.make_async_copy(rpool_ref.at[pid], rbuf.at[sl, :, lanes], sem.at[1, sl]))

    def start_batch(batch, sl):
        def body(pg, c):
            for cp in page_copies(batch, sl, pg):
                cp.start()
            return c
        lax.fori_loop(0, n_pages, body, 0)

    @pl.when(b == 0)
    def _():
        start_batch(0, 0)

    @pl.when(b + 1 < pl.num_programs(0))
    def _():
        start_batch(b + 1, 1 - slot)

    def wait_body(pg, c):
        for cp in page_copies(b, slot, pg):
            cp.wait()
        return c
    lax.fori_loop(0, n_pages, wait_body, 0)

    q = q_ref[0]
    rows = q.shape[0]
    c_past = cbuf[slot].reshape(n_pages * PAGE, C_KV_LORA).astype(BF16)
    r_past = rbuf[slot].astype(BF16)
    s = (_dot_nt(q[:, :C_KV_LORA], c_past)
         + _dot(q[:, C_KV_LORA:C_KV_LORA + C_ROPE], r_past)) * MLA_SCALE
    qf = q.astype(F32)
    knf = kn_ref[0].astype(F32)
    tok = lax.broadcasted_iota(jnp.int32, (rows, 1), 0) % t_new
    s_n = []
    for t in range(t_new):
        sc = jnp.sum(qf * knf[t:t + 1, :], axis=-1, keepdims=True) * MLA_SCALE
        s_n.append(jnp.where(tok >= t, sc, -jnp.inf))
    m = jnp.max(s, axis=-1, keepdims=True)
    for sc in s_n:
        m = jnp.maximum(m, sc)
    p = jnp.exp(s - m)
    l = jnp.sum(p, axis=-1, keepdims=True)
    acc = _dot(p.astype(BF16), c_past)
    for t in range(t_new):
        e = jnp.exp(s_n[t] - m)
        l = l + e
        acc = acc + e * knf[t:t + 1, :C_KV_LORA]
    o_ref[0] = (acc / l).astype(o_ref.dtype)


def _mla_decode(q_rows, k_new, c_pool, r_pool_t, page_table_t):
    b, rows, _ = q_rows.shape
    t_new = k_new.shape[1]
    n_pages = page_table_t.shape[0]
    return pl.pallas_call(
        functools.partial(_mla_dec_kernel, n_pages=n_pages, t_new=t_new),
        grid_spec=pltpu.PrefetchScalarGridSpec(
            num_scalar_prefetch=1,
            grid=(b,),
            in_specs=[pl.BlockSpec((1, rows, C_CAT), lambda i, *_: (i, 0, 0)),
                      pl.BlockSpec((1, t_new, C_CAT), lambda i, *_: (i, 0, 0)),
                      pl.BlockSpec(memory_space=pl.ANY),
                      pl.BlockSpec(memory_space=pl.ANY)],
            out_specs=pl.BlockSpec((1, rows, C_KV_LORA), lambda i, *_: (i, 0, 0)),
            scratch_shapes=[pltpu.VMEM((2, n_pages, PAGE, C_KV_LORA), F32),
                            pltpu.VMEM((2, C_ROPE, n_pages * PAGE), F32),
                            pltpu.SemaphoreType.DMA((2, 2))]),
        out_shape=jax.ShapeDtypeStruct((b, rows, C_KV_LORA), BF16),
        compiler_params=_params("arbitrary"),
        name="mla_decode",
    )(page_table_t, q_rows, k_new, c_pool, r_pool_t)


def _mla_out_kernel(o_ref, wuv_ref, wo_ref, x_ref, y_ref):
    y_ref[...] = _mla_out([o_ref[hd] for hd in range(C_HEADS)], wuv_ref, wo_ref, x_ref[...])


def _mla_out_proj(o_heads, wuv, wo, x):
    m, d = x.shape
    return pl.pallas_call(
        _mla_out_kernel,
        grid=(1,),
        in_specs=[_resident(o_heads.shape), _resident(wuv.shape), _resident(wo.shape), _resident(x.shape)],
        out_specs=pl.BlockSpec((m, d), lambda i: (0, 0)),
        out_shape=jax.ShapeDtypeStruct((m, d), F32),
        compiler_params=_params("arbitrary"),
        name="mla_out_proj",
    )(o_heads, wuv, wo, x)


def _ffn_kernel(*refs, tm, tiles_per_seq, seq_in_tile, final_norm):
    x_ref, g_ref, wup_ref, wconv_ref, bconv_ref, wdown_ref = refs[:6]
    refs = refs[6:]
    if seq_in_tile:
        e1_ref, e2_ref = refs[:2]
        refs = refs[2:]
    if final_norm:
        gfin_ref = refs[0]
        refs = refs[1:]
    y_ref, gate_out_ref, gbuf = refs
    halo = 8

    if seq_in_tile:
        gbuf[0:halo, :] = jnp.zeros((halo, D_FF), F32)
    else:
        @pl.when(pl.program_id(0) % tiles_per_seq == 0)
        def _():
            gbuf[0:halo, :] = jnp.zeros((halo, D_FF), F32)

    x = x_ref[...]
    h = _rms(x, g_ref[...]).astype(BF16)
    gate = _dot(h, wup_ref[:, :D_FF])
    up = _dot(h, wup_ref[:, D_FF:])
    gbuf[halo:halo + tm, :] = gate
    tap0 = gbuf[halo - 2:halo - 2 + tm, :]
    tap1 = gbuf[halo - 1:halo - 1 + tm, :]
    if seq_in_tile:
        tok = lax.broadcasted_iota(jnp.int32, (tm, 1), 0) % seq_in_tile
        tap0 = jnp.where(tok < 2, e2_ref[...], tap0)
        tap1 = jnp.where(tok < 1, e1_ref[...], tap1)
        gate_out_ref[...] = gate
    else:
        gate_out_ref[0] = gate[tm - halo:, :]
        gbuf[0:halo, :] = gate[tm - halo:, :]
    conv = tap0 * wconv_ref[0:1, :] + bconv_ref[...] + tap1 * wconv_ref[1:2, :] + gate * wconv_ref[2:3, :]
    act = (jax.nn.silu(conv) * up).astype(BF16)
    y = x + _dot(act, wdown_ref[...])
    if final_norm:
        y = _rms(y, gfin_ref[...])
    y_ref[...] = y


def _conv_ffn(x, g, wup, wconv, bconv, wdown, *, tm, seq, conv_state=None, g_final=None):
    m, d = x.shape
    seq_in_tile = 0 if conv_state is None else seq
    row = pl.BlockSpec((tm, d), lambda i: (i, 0))
    wide = pl.BlockSpec((tm, D_FF), lambda i: (i, 0))
    args = [x, g.reshape(1, d), wup, wconv, bconv.reshape(1, D_FF), wdown]
    in_specs = [row, _resident((1, d)), _resident(wup.shape), _resident(wconv.shape),
                _resident((1, D_FF)), _resident(wdown.shape)]
    if conv_state is not None:
        assert tm % seq == 0 and seq >= 2
        zeros = jnp.zeros((m // seq, seq - 2, D_FF), F32)
        e2 = jnp.concatenate([conv_state, zeros], axis=1).reshape(m, D_FF)
        e1 = jnp.concatenate([conv_state[:, 1:], zeros, zeros[:, :1]], axis=1).reshape(m, D_FF)
        args += [e1, e2]
        in_specs += [wide, wide]
        gate_spec = wide
        gate_shape = jax.ShapeDtypeStruct((m, D_FF), F32)
    else:
        assert seq % tm == 0
        gate_spec = pl.BlockSpec((1, 8, D_FF), lambda i: (i // (seq // tm), 0, 0))
        gate_shape = jax.ShapeDtypeStruct((m // seq, 8, D_FF), F32)
    if g_final is not None:
        args.append(g_final.reshape(1, d))
        in_specs.append(_resident((1, d)))
    return pl.pallas_call(
        functools.partial(_ffn_kernel, tm=tm, tiles_per_seq=max(seq // tm, 1), seq_in_tile=seq_in_tile,
                          final_norm=g_final is not None),
        grid=(m // tm,),
        in_specs=in_specs,
        out_specs=[row, gate_spec],
        out_shape=[jax.ShapeDtypeStruct((m, d), F32), gate_shape],
        scratch_shapes=[pltpu.VMEM((tm + 8, D_FF), F32)],
        compiler_params=_params("arbitrary"),
        name="conv_ffn",
    )(*args)


def _prepare_weights(w_in_ab, w_out_ab, w_dq, w_uq, w_dkv, w_uk, w_uv, w_o_c, w_up, w_down):
    per_head = C_NOPE + C_ROPE
    uq = w_uq[0].reshape(C_Q_LORA, C_HEADS, per_head)
    wuqr = jnp.pad(uq[:, :, C_NOPE:], ((0, 0), (0, 0), (0, 128 - C_ROPE)))
    return dict(
        w_in=w_in_ab[0].astype(BF16),
        w_out=w_out_ab[0].astype(BF16),
        wdq=w_dq[0].astype(BF16),
        wuqn=uq[:, :, :C_NOPE].reshape(C_Q_LORA, C_HEADS * C_NOPE).astype(BF16),
        wuqr=wuqr.reshape(C_Q_LORA, C_HEADS * 128).astype(BF16),
        wdkv=jnp.pad(w_dkv[0], ((0, 0), (0, 128 - C_ROPE))).astype(BF16),
        wukt=w_uk[0].transpose(1, 2, 0).astype(BF16),
        wuv=w_uv[0].transpose(1, 0, 2).astype(BF16),
        wo=w_o_c[0].astype(BF16),
        w_up=w_up.astype(BF16),
        w_down=w_down.astype(BF16),
    )


def _alibi_slopes():
    return 2.0 ** (-8.0 * (jnp.arange(A_HEADS, dtype=F32) + 1.0) / A_HEADS)


AB_WIDTHS = (A_WIDTH,) * 3 + (B_WIDTH,) * 4


def _prompt_trunk(x, w, slopes, g_mix, g_ffn, g_final, g_ret, g_q, g_kv, w_conv, b_conv):
    b, s, d = x.shape
    m = b * s
    pos = jnp.arange(s)
    qa, ka, va, qb, kb, vb, gb, ka_t, va_t = _norm_proj(x, g_mix[0], w["w_in"], AB_WIDTHS, tm=512,
                                                        transposed=(1, 2))
    x = x.reshape(m, d)
    o_a = _attn_a_prompt(qa, ka, va, slopes)
    o_b, ret = _retention(qb, kb, vb, gb, pos, g_ret[0], jnp.zeros((b, B_HEADS, B_DIM, B_DIM), F32),
                          RET_CHUNK, rows=512)
    x = _out_proj(o_a.reshape(m, -1), o_b.reshape(m, -1), w["w_out"], x, tm=min(1024, m))
    tm_ffn = 256
    x, tails0 = _conv_ffn(x, g_ffn[0], w["w_up"][0], w_conv[0], b_conv[0], w["w_down"][0], tm=tm_ffn, seq=s)
    qcat, kcat, c_kv, k_rope_t = _mla_prep(x.reshape(b, s, d), pos, g_mix[1], w["wdq"], g_q[0], w["wuqn"],
                                           w["wuqr"], w["wdkv"], g_kv[0], w["wukt"], tm=256)
    x = _mla_flash(qcat, kcat, x.reshape(b, s, d), w["wuv"], w["wo"], tq=256, tk=512).reshape(m, d)
    y, tails1 = _conv_ffn(x, g_ffn[1], w["w_up"][1], w_conv[1], b_conv[1], w["w_down"][1], tm=tm_ffn, seq=s,
                          g_final=g_final)
    new_conv = jnp.stack([tails0[:, 6:8], tails1[:, 6:8]])
    keep = min(A_PATTERNS[-1][0], s)
    heads = lambda a: a[:, :, s - keep:].reshape(b, A_HEADS, A_HEAD_DIM, keep).transpose(0, 3, 1, 2)[None]
    return (y.reshape(b, s, d), heads(ka_t), heads(va_t), ret[None], c_kv[None],
            k_rope_t.transpose(0, 2, 1)[None], new_conv)


def _sample_trunk(x, w, slopes, g_mix, g_ffn, g_final, g_ret, g_q, g_kv, w_conv, b_conv,
                  cache_k, cache_v, state_ret, conv_state, c_pool, r_pool, page_table):
    b, t, d = x.shape
    m = b * t
    pos = PAST_LEN + jnp.arange(t)
    x = x.reshape(m, d)
    qa, ka, va, qb, kb, vb, gb = [a.reshape(b, t, -1) for a in
                                  _norm_proj(x.reshape(1, m, d), g_mix[0], w["w_in"], AB_WIDTHS, tm=m)]
    nbuf = cache_k.shape[1]
    feature_major = lambda a: a.transpose(0, 2, 3, 1).reshape(b, A_WIDTH, nbuf)
    o_a, new_k, new_v = _attn_a_decode(qa, ka, va, feature_major(cache_k), feature_major(cache_v), slopes)
    padr = lambda a: jnp.pad(a, ((0, 0), (0, RET_CHUNK - t), (0, 0)))
    o_b, ret = _retention(padr(qb), padr(kb), padr(vb), padr(gb), jnp.pad(pos, (0, RET_CHUNK - t)),
                          g_ret[0], state_ret, t, rows=RET_CHUNK)
    x = _out_proj(o_a.reshape(m, -1), o_b[:, :t].reshape(m, -1), w["w_out"], x, tm=m)
    x, gate0 = _conv_ffn(x, g_ffn[0], w["w_up"][0], w_conv[0], b_conv[0], w["w_down"][0], tm=m, seq=t,
                         conv_state=conv_state[0])
    qcat, kcat, c_kv, k_rope_t = _mla_prep(x.reshape(1, m, d), jnp.tile(pos, b), g_mix[1], w["wdq"], g_q[0],
                                           w["wuqn"], w["wuqr"], w["wdkv"], g_kv[0], w["wukt"], tm=m)
    q_rows = qcat.reshape(C_HEADS, b, t, C_CAT).transpose(1, 0, 2, 3).reshape(b, C_HEADS * t, C_CAT)
    o_lat = _mla_decode(q_rows, kcat.reshape(b, t, C_CAT), c_pool, r_pool.transpose(0, 2, 1), page_table.T)
    o_heads = o_lat.reshape(b, C_HEADS, t, C_KV_LORA).transpose(1, 0, 2, 3).reshape(C_HEADS, m, C_KV_LORA)
    x = _mla_out_proj(o_heads, w["wuv"], w["wo"], x)
    y, gate1 = _conv_ffn(x, g_ffn[1], w["w_up"][1], w_conv[1], b_conv[1], w["w_down"][1], tm=m, seq=t,
                         conv_state=conv_state[1], g_final=g_final)
    new_conv = jnp.stack([gate0.reshape(b, t, D_FF)[:, t - 2:], gate1.reshape(b, t, D_FF)[:, t - 2:]])
    heads = lambda a: a.reshape(b, A_HEADS, A_HEAD_DIM, nbuf).transpose(0, 3, 1, 2)[None]
    return (y.reshape(b, t, d), heads(new_k), heads(new_v), ret[None], c_kv.reshape(1, b, t, C_KV_LORA),
            k_rope_t.reshape(C_ROPE, b, t).transpose(1, 2, 0)[None], new_conv)


def kernel(x_prompt, x_sample, cache_a_k, cache_a_v, state_ret, cache_c_kv, cache_k_rope, state_ffn_conv,
           page_table, g_mix, g_ffn, g_final, w_in_ab, w_out_ab, g_ret, w_dq, g_q, w_uq, w_dkv, g_kv,
           w_uk, w_uv, w_o_c, w_up, w_conv, b_conv, w_down):
    assert g_mix.shape[0] == 2 and w_in_ab.shape[0] == 1 and w_dq.shape[0] == 1
    w = _prepare_weights(w_in_ab, w_out_ab, w_dq, w_uq, w_dkv, w_uk, w_uv, w_o_c, w_up, w_down)
    shared = (w, _alibi_slopes(), g_mix, g_ffn, g_final, g_ret, g_q, g_kv, w_conv, b_conv)
    out_p = _prompt_trunk(x_prompt, *shared)
    out_s = _sample_trunk(x_sample, *shared, cache_a_k[0], cache_a_v[0], state_ret[0], state_ffn_conv,
                          cache_c_kv[0], cache_k_rope[0], page_table)
    return (out_p[0], out_s[0]) + out_p[1:] + out_s[1:]
```

```python
import functools

import jax
import jax.numpy as jnp
from jax import lax
from jax.experimental import pallas as pl
from jax.experimental.pallas import tpu as pltpu

F32 = jnp.float32
BF16 = jnp.bfloat16

D_MODEL = 1024
A_HEADS = 8
A_HEAD_DIM = 64
A_WIDTH = A_HEADS * A_HEAD_DIM
A_PATTERNS = ((128, 1), (512, 4), (2048, 16))
BAND = 128
B_HEADS = 4
B_DIM = 128
B_WIDTH = B_HEADS * B_DIM
RET_CHUNK = 128
C_HEADS = 8
C_Q_LORA = 384
C_KV_LORA = 256
C_NOPE = 128
C_ROPE = 64
C_V_DIM = 128
C_CAT = C_KV_LORA + 128
MLA_SCALE = (C_NOPE + C_ROPE) ** -0.5
D_FF = 2816
ROPE_BASE = 10000.0
EPS = 1e-6
PAST_LEN = 8192
PAGE = 128

VMEM_LIMIT = 56 * 1024 * 1024

NT_DIMS = (((1,), (1,)), ((), ()))
TN_DIMS = (((0,), (0,)), ((), ()))


def _params(*sem):
    return pltpu.CompilerParams(dimension_semantics=sem, vmem_limit_bytes=VMEM_LIMIT)


def _resident(shape):
    nd = len(shape)
    return pl.BlockSpec(shape, lambda *_: (0,) * nd, pipeline_mode=pl.Buffered(1))


def _rms(x, g):
    return x * lax.rsqrt(jnp.mean(x * x, axis=-1, keepdims=True) + EPS) * g


def _dot(a, b):
    return jnp.dot(a, b, preferred_element_type=F32)


def _dot_nt(a, b):
    return lax.dot_general(a, b, NT_DIMS, preferred_element_type=F32)


def _norm_proj_kernel(x_ref, g_ref, w_ref, *out_refs, n_groups, transposed):
    h = _rms(x_ref[0], g_ref[...]).astype(BF16)
    t_refs = dict(zip(transposed, out_refs[n_groups:]))
    off = 0
    for grp, o_ref in enumerate(out_refs[:n_groups]):
        n = o_ref.shape[-1]
        r = _dot(h, w_ref[:, off:off + n])
        o_ref[0] = r
        if grp in t_refs:
            t_refs[grp][0] = r.T
        off += n


def _norm_proj(x, g, w, widths, tm, transposed=()):
    b, s, d = x.shape
    return pl.pallas_call(
        functools.partial(_norm_proj_kernel, n_groups=len(widths), transposed=tuple(transposed)),
        grid=(b, s // tm),
        in_specs=[pl.BlockSpec((1, tm, d), lambda i, j: (i, j, 0)),
                  _resident((1, d)),
                  _resident(w.shape)],
        out_specs=([pl.BlockSpec((1, tm, n), lambda i, j: (i, j, 0)) for n in widths]
                   + [pl.BlockSpec((1, widths[t], tm), lambda i, j: (i, 0, j)) for t in transposed]),
        out_shape=([jax.ShapeDtypeStruct((b, s, n), F32) for n in widths]
                   + [jax.ShapeDtypeStruct((b, widths[t], s), F32) for t in transposed]),
        compiler_params=_params("parallel", "parallel"),
        name="norm_proj",
    )(x, g.reshape(1, d), w)


def _unrolled_loop(n_items, fn):
    unroll = next(u for u in (4, 3, 2, 1) if n_items % u == 0)
    if n_items <= unroll:
        for t in range(n_items):
            fn(t)
        return

    def body(it, carry):
        for u in range(unroll):
            fn(it * unroll + u)
        return carry

    lax.fori_loop(0, n_items // unroll, body, 0)


def _attn_a_kernel(slopes_ref, q_ref, k_ref, v_ref, o_ref, os_ref, ls_ref, bias_ref, *, seq):
    pair = pl.program_id(1)
    is_a = lax.broadcasted_iota(jnp.int32, (1, 128), 1) < A_HEAD_DIM

    row = lax.broadcasted_iota(jnp.int32, (2 * BAND, 2 * BAND), 0)
    col = lax.broadcasted_iota(jnp.int32, (2 * BAND, 2 * BAND), 1)
    rel = BAND + (row & (BAND - 1)) - col
    slope = jnp.where(row < BAND, slopes_ref[2 * pair], slopes_ref[2 * pair + 1])
    in_band = (rel >= 0) & (rel <= BAND)
    for pat, (_, dil) in enumerate(A_PATTERNS):
        bias_ref[pat] = jnp.where(in_band, -slope * (rel * dil).astype(F32), -jnp.inf)

    def block(pat, dil, q_start, k_start, first):
        nk = BAND if first else 2 * BAND
        q = q_ref[0, pl.ds(q_start, BAND, stride=dil), :] * (A_HEAD_DIM ** -0.5)
        k = k_ref[0, pl.ds(k_start, nk, stride=dil), :].astype(BF16)
        v = v_ref[0, pl.ds(k_start, nk, stride=dil), :].astype(BF16)
        q2 = jnp.concatenate([jnp.where(is_a, q, 0.0), jnp.where(is_a, 0.0, q)], axis=0).astype(BF16)
        s = _dot_nt(q2, k) + (bias_ref[pat, :, BAND:] if first else bias_ref[pat])
        m = jnp.max(s, axis=-1, keepdims=True)
        e = jnp.exp(s - m)
        l = jnp.sum(e, axis=-1, keepdims=True)
        o = _dot(e.astype(BF16), v) / l
        lse = m + jnp.log(l)
        os_ref[pat, pl.ds(q_start, BAND, stride=dil), :] = jnp.where(is_a, o[:BAND], o[BAND:])
        ls_ref[pat, pl.ds(q_start, BAND, stride=dil), :] = jnp.where(is_a, lse[:BAND], lse[BAND:])

    for pat, (_, dil) in enumerate(A_PATTERNS):
        nb = seq // dil // BAND

        def first_block(r, pat=pat, dil=dil):
            block(pat, dil, r, r, True)

        def later_block(t, pat=pat, dil=dil, nb=nb):
            r = t // (nb - 1)
            j = 1 + t % (nb - 1)
            block(pat, dil, r + dil * BAND * j, r + dil * BAND * (j - 1), False)

        _unrolled_loop(dil, first_block)
        if nb > 1:
            _unrolled_loop(dil * (nb - 1), later_block)

    rows = 256

    def combine(i, carry):
        sl = pl.ds(pl.multiple_of(i * rows, rows), rows)
        l0, l1, l2 = ls_ref[0, sl, :], ls_ref[1, sl, :], ls_ref[2, sl, :]
        mx = jnp.maximum(jnp.maximum(l0, l1), l2)
        w0, w1, w2 = jnp.exp(l0 - mx), jnp.exp(l1 - mx), jnp.exp(l2 - mx)
        num = w0 * os_ref[0, sl, :] + w1 * os_ref[1, sl, :] + w2 * os_ref[2, sl, :]
        o_ref[0, sl, :] = (num / (w0 + w1 + w2)).astype(o_ref.dtype)
        return carry

    lax.fori_loop(0, seq // rows, combine, 0)


def _attn_a_prompt(q, k, v, slopes):
    b, s, _ = q.shape
    assert s % (A_PATTERNS[-1][1] * BAND) == 0
    spec = pl.BlockSpec((1, s, 128), lambda i, p, *_: (i, 0, p))
    return pl.pallas_call(
        functools.partial(_attn_a_kernel, seq=s),
        grid_spec=pltpu.PrefetchScalarGridSpec(
            num_scalar_prefetch=1,
            grid=(b, A_WIDTH // 128),
            in_specs=[spec, spec, spec],
            out_specs=spec,
            scratch_shapes=[pltpu.VMEM((3, s, 128), F32), pltpu.VMEM((3, s, 128), F32),
                            pltpu.VMEM((3, 2 * BAND, 2 * BAND), F32)]),
        out_shape=jax.ShapeDtypeStruct((b, s, A_WIDTH), BF16),
        compiler_params=_params("parallel", "parallel"),
        name="attn_a_prompt",
    )(slopes, q, k, v)


def _pattern_multiplicity(delta):
    mult = jnp.zeros(delta.shape, F32)
    for window, dil in A_PATTERNS:
        hit = (delta >= 0) & (delta <= window)
        if dil > 1:
            hit = hit & ((delta & (dil - 1)) == 0)
        mult = mult + jnp.where(hit, 1.0, 0.0)
    return mult


def _attn_a_dec_kernel(slopes_ref, q_ref, kn_ref, vn_ref, knt_ref, vnt_ref, ck_ref, cv_ref,
                       o_ref, nk_ref, nv_ref, *, nbuf, t_new):
    half = pl.program_id(1)
    width = q_ref.shape[-1]
    heads = width // A_HEAD_DIM
    rows = heads * t_new
    q = q_ref[0]
    q_rows = jnp.concatenate([q] * heads, axis=0)
    row = lax.broadcasted_iota(jnp.int32, (rows, width), 0)
    lane = lax.broadcasted_iota(jnp.int32, (rows, width), 1)
    qm = jnp.where(row // t_new == lane // A_HEAD_DIM, q_rows, 0.0).astype(BF16)

    rcol = lax.broadcasted_iota(jnp.int32, (rows, 1), 0)
    slope = jnp.zeros((rows, 1), F32)
    for h in range(heads):
        slope = jnp.where(rcol // t_new == h, slopes_ref[half * heads + h], slope)
    tok = rcol % t_new

    s_c = _dot(qm, ck_ref[0].astype(BF16)) * (A_HEAD_DIM ** -0.5)
    pos = lax.broadcasted_iota(jnp.int32, (rows, nbuf), 1)
    delta_c = nbuf + tok - pos
    mult_c = _pattern_multiplicity(delta_c)
    s_c = jnp.where(mult_c > 0, s_c - slope * delta_c.astype(F32), -jnp.inf)
    qf = qm.astype(F32)
    knf = kn_ref[0].astype(BF16).astype(F32)
    vnf = vn_ref[0].astype(BF16).astype(F32)
    s_n, mult_n = [], []
    for t in range(t_new):
        delta = tok - t
        mult = _pattern_multiplicity(delta)
        sc = jnp.sum(qf * knf[t:t + 1, :], axis=-1, keepdims=True) * (A_HEAD_DIM ** -0.5)
        s_n.append(jnp.where(mult > 0, sc - slope * delta.astype(F32), -jnp.inf))
        mult_n.append(mult)
    m = jnp.max(s_c, axis=-1, keepdims=True)
    for sc in s_n:
        m = jnp.maximum(m, sc)
    e_c = jnp.exp(s_c - m) * mult_c
    l = jnp.sum(e_c, axis=-1, keepdims=True)
    acc = _dot_nt(e_c.astype(BF16), cv_ref[0].astype(BF16))
    for t in range(t_new):
        e = jnp.exp(s_n[t] - m) * mult_n[t]
        l = l + e
        acc = acc + e * vnf[t:t + 1, :]
    res = acc / l
    lane_head = lax.broadcasted_iota(jnp.int32, (t_new, width), 1) // A_HEAD_DIM
    out = jnp.zeros((t_new, width), F32)
    for h in range(heads):
        out = jnp.where(lane_head == h, res[h * t_new:(h + 1) * t_new], out)
    o_ref[0] = out.astype(o_ref.dtype)

    is_new = lax.broadcasted_iota(jnp.int32, (width, 128), 1) >= 128 - t_new
    for c_ref, nt_ref, n_ref in ((ck_ref, knt_ref, nk_ref), (cv_ref, vnt_ref, nv_ref)):
        rolled = pltpu.roll(c_ref[0], nbuf - t_new, 1)
        n_ref[0, :, 0:nbuf - 128] = rolled[:, 0:nbuf - 128]
        n_ref[0, :, nbuf - 128:nbuf] = jnp.where(is_new, nt_ref[0], rolled[:, nbuf - 128:nbuf])


def _attn_a_decode(q, kn, vn, cache_kt, cache_vt, slopes):
    b, t_new, _ = q.shape
    nbuf = cache_kt.shape[2]
    width = 256
    tail = lambda a: jnp.pad(a.transpose(0, 2, 1), ((0, 0), (0, 0), (128 - t_new, 0)))
    small = pl.BlockSpec((1, t_new, width), lambda i, c, *_: (i, 0, c))
    newt = pl.BlockSpec((1, width, 128), lambda i, c, *_: (i, c, 0))
    big = pl.BlockSpec((1, width, nbuf), lambda i, c, *_: (i, c, 0))
    return pl.pallas_call(
        functools.partial(_attn_a_dec_kernel, nbuf=nbuf, t_new=t_new),
        grid_spec=pltpu.PrefetchScalarGridSpec(
            num_scalar_prefetch=1,
            grid=(b, A_WIDTH // width),
            in_specs=[small, small, small, newt, newt, big, big],
            out_specs=[small, big, big]),
        out_shape=[jax.ShapeDtypeStruct((b, t_new, A_WIDTH), BF16),
                   jax.ShapeDtypeStruct(cache_kt.shape, F32),
                   jax.ShapeDtypeStruct(cache_vt.shape, F32)],
        compiler_params=_params("parallel", "parallel"),
        name="attn_a_decode",
    )(slopes, q, kn, vn, tail(kn), tail(vn), cache_kt, cache_vt)


def _retention_head(q, k, v, gate, rope_q, rope_k, dmask, qdec, kdec, cdec, g, state):
    (cos_q, sin_q), (cos_k, sin_k) = rope_q, rope_k
    qr = (q * cos_q + pltpu.roll(q, B_DIM // 2, 1) * sin_q).astype(BF16)
    kr = (k * cos_k + pltpu.roll(k, B_DIM // 2, 1) * sin_k) * (B_DIM ** -0.5)
    v = v.astype(BF16)
    att = _dot_nt(qr, kr.astype(BF16)) * dmask
    o = _dot(att.astype(BF16), v) + _dot(qr, state.astype(BF16)) * qdec
    kd = (kr * kdec).astype(BF16)
    new_state = state * cdec + lax.dot_general(kd, v, TN_DIMS, preferred_element_type=F32)
    d = o - jnp.mean(o, axis=-1, keepdims=True)
    y = d * lax.rsqrt(jnp.mean(d * d, axis=-1, keepdims=True) + EPS) * g
    return y * jax.nn.silu(gate), new_state


def _retention_kernel(q_ref, k_ref, v_ref, gate_ref, cos_ref, sin_ref, dmask_ref, qdec_ref, kdec_ref,
                      cdec_ref, g_ref, s0_ref, o_ref, s_ref, *, chunk, n_chunks):
    @pl.when(pl.program_id(1) == 0)
    def _():
        s_ref[...] = s0_ref[...]

    for c in range(n_chunks):
        rows = slice(c * chunk, (c + 1) * chunk)
        rope = (cos_ref[rows, :], sin_ref[rows, :])
        for h in range(B_HEADS):
            cols = slice(h * B_DIM, (h + 1) * B_DIM)
            y, s_ref[0, h] = _retention_head(
                q_ref[0, rows, cols], k_ref[0, rows, cols], v_ref[0, rows, cols], gate_ref[0, rows, cols],
                rope, rope, dmask_ref[h], qdec_ref[h], kdec_ref[h], cdec_ref[h], g_ref[:, cols], s_ref[0, h])
            o_ref[0, rows, cols] = y.astype(o_ref.dtype)


def _retention_dec_kernel(q_ref, k_ref, v_ref, gate_ref, cosq_ref, sinq_ref, cosk_ref, sink_ref, dmask_ref,
                          qdec_ref, kdec_ref, cdec_ref, g_ref, s0_ref, o_ref, s_ref, kpad, vpad, *, n_seq):
    t_rows = k_ref.shape[1]
    kpad[...] = jnp.zeros(kpad.shape, F32)
    vpad[...] = jnp.zeros(vpad.shape, F32)
    rope_q = (cosq_ref[...], sinq_ref[...])
    rope_k = (cosk_ref[...], sink_ref[...])
    for b in range(n_seq):
        kpad[b, 0:t_rows, :] = k_ref[b]
        vpad[b, 0:t_rows, :] = v_ref[b]
        for h in range(B_HEADS):
            cols = slice(h * B_DIM, (h + 1) * B_DIM)
            y, s_ref[b, h] = _retention_head(
                q_ref[b, :, cols], kpad[b, :, cols], vpad[b, :, cols], gate_ref[b, :, cols],
                rope_q, rope_k, dmask_ref[h], qdec_ref[h], kdec_ref[h], cdec_ref[h], g_ref[:, cols], s0_ref[b, h])
            o_ref[b, :, cols] = y.astype(o_ref.dtype)


def _retention_tables(chunk, q_pad, k_pad):
    lg = jnp.log(1.0 - 2.0 ** (-5.0 - jnp.arange(B_HEADS, dtype=F32)))
    qi = jnp.arange(q_pad, dtype=F32)
    ki = jnp.arange(k_pad, dtype=F32)
    rel = qi[:, None] - ki[None, :]
    live = (rel >= 0) & (qi < chunk)[:, None] & (ki < chunk)[None, :]
    dmask = jnp.where(live, jnp.exp(jnp.maximum(rel, 0.0)[None] * lg[:, None, None]), 0.0)
    qdec = jnp.where((qi < chunk)[None, :], jnp.exp((qi + 1.0)[None, :] * lg[:, None]), 0.0)
    kdec = jnp.where((ki < chunk)[None, :], jnp.exp((chunk - 1.0 - ki)[None, :] * lg[:, None]), 0.0)
    cdec = jnp.exp(chunk * lg)
    bc = lambda a: jnp.broadcast_to(a[..., None], a.shape + (B_DIM,))
    return dmask, bc(qdec), bc(kdec), bc(cdec[:, None])


def _rope_tables(pos, dim, width):
    half = dim // 2
    inv = ROPE_BASE ** (-jnp.arange(half, dtype=F32) / half)
    ang = pos.astype(F32)[:, None] * inv[None, :]
    cos, sin = jnp.cos(ang), jnp.sin(ang)
    zeros = jnp.zeros((pos.shape[0], width - dim), F32)
    return (jnp.concatenate([cos, cos, zeros], axis=1), jnp.concatenate([-sin, sin, zeros], axis=1))


def _retention_decode(q, k, v, gate, pos, g_ret, s0, n_seq):
    b, t, _ = q.shape
    t_pad = 8
    assert t <= t_pad and b % n_seq == 0
    pad_rows = lambda a: jnp.pad(a, ((0, 0), (0, t_pad - t), (0, 0)))
    cosk, sink = _rope_tables(jnp.pad(pos, (0, RET_CHUNK - t)), B_DIM, B_DIM)
    dmask, qdec, kdec, cdec = _retention_tables(t, t_pad, RET_CHUNK)
    tok = pl.BlockSpec((n_seq, t_pad, B_WIDTH), lambda i: (i, 0, 0))
    st = pl.BlockSpec((n_seq, B_HEADS, B_DIM, B_DIM), lambda i: (i, 0, 0, 0))
    consts = (cosk[:t_pad], sink[:t_pad], cosk, sink, dmask, qdec, kdec, cdec, g_ret.reshape(1, B_WIDTH))
    o, s_new = pl.pallas_call(
        functools.partial(_retention_dec_kernel, n_seq=n_seq),
        grid=(b // n_seq,),
        in_specs=[tok, tok, tok, tok] + [_resident(c.shape) for c in consts] + [st],
        out_specs=[tok, st],
        out_shape=[jax.ShapeDtypeStruct((b, t_pad, B_WIDTH), BF16),
                   jax.ShapeDtypeStruct((b, B_HEADS, B_DIM, B_DIM), F32)],
        scratch_shapes=[pltpu.VMEM((n_seq, RET_CHUNK, B_WIDTH), F32),
                        pltpu.VMEM((n_seq, RET_CHUNK, B_WIDTH), F32)],
        compiler_params=_params("parallel"),
        name="retention_decode",
    )(pad_rows(q), pad_rows(k), pad_rows(v), pad_rows(gate), *consts, s0)
    return o[:, :t], s_new


def _retention(q, k, v, gate, pos, g_ret, s0, rows):
    b, s, _ = q.shape
    cos, sin = _rope_tables(pos, B_DIM, B_DIM)
    dmask, qdec, kdec, cdec = _retention_tables(RET_CHUNK, RET_CHUNK, RET_CHUNK)
    tok = pl.BlockSpec((1, rows, B_WIDTH), lambda i, j: (i, j, 0))
    tab = pl.BlockSpec((rows, B_DIM), lambda i, j: (j, 0))
    st = pl.BlockSpec((1, B_HEADS, B_DIM, B_DIM), lambda i, j: (i, 0, 0, 0))
    return pl.pallas_call(
        functools.partial(_retention_kernel, chunk=RET_CHUNK, n_chunks=rows // RET_CHUNK),
        grid=(b, s // rows),
        in_specs=[tok, tok, tok, tok, tab, tab, _resident(dmask.shape), _resident(qdec.shape),
                  _resident(kdec.shape), _resident(cdec.shape), _resident((1, B_WIDTH)), st],
        out_specs=[tok, st],
        out_shape=[jax.ShapeDtypeStruct((b, s, B_WIDTH), BF16),
                   jax.ShapeDtypeStruct((b, B_HEADS, B_DIM, B_DIM), F32)],
        compiler_params=_params("parallel", "arbitrary"),
        name="retention",
    )(q, k, v, gate, cos, sin, dmask, qdec, kdec, cdec, g_ret.reshape(1, B_WIDTH), s0)


def _out_proj_kernel(a_ref, b_ref, w_ref, x_ref, y_ref):
    na = a_ref.shape[-1]
    y_ref[...] = x_ref[...] + _dot(a_ref[...], w_ref[:na, :]) + _dot(b_ref[...], w_ref[na:, :])


def _out_proj(a, b, w, x, tm):
    m, d = x.shape
    return pl.pallas_call(
        _out_proj_kernel,
        grid=(m // tm,),
        in_specs=[pl.BlockSpec((tm, a.shape[1]), lambda i: (i, 0)),
                  pl.BlockSpec((tm, b.shape[1]), lambda i: (i, 0)),
                  _resident(w.shape),
                  pl.BlockSpec((tm, d), lambda i: (i, 0))],
        out_specs=pl.BlockSpec((tm, d), lambda i: (i, 0)),
        out_shape=jax.ShapeDtypeStruct((m, d), F32),
        compiler_params=_params("parallel"),
        name="out_proj",
    )(a, b, w, x)


def _rope_group(x, cos, sin):
    return x * cos + (pltpu.roll(x, C_ROPE // 2, 1) + pltpu.roll(x, 128 - C_ROPE // 2, 1)) * sin


def _mla_prep_kernel(x_ref, g_ref, wdq_ref, gq_ref, wuqn_ref, wuqr_ref, wdkv_ref, gkv_ref, wukt_ref,
                     cos_ref, sin_ref, qcat_ref, kcat_ref, ckv_ref, kr_ref):
    h = _rms(x_ref[0], g_ref[...]).astype(BF16)
    cq = _rms(_dot(h, wdq_ref[...]), gq_ref[...]).astype(BF16)
    q_nope = _dot(cq, wuqn_ref[...])
    q_rope = _dot(cq, wuqr_ref[...])
    kv = _dot(h, wdkv_ref[...])
    cos = cos_ref[...]
    sin = sin_ref[...]
    c_kv = _rms(kv[:, :C_KV_LORA], gkv_ref[...])
    k_rope = _rope_group(kv[:, C_KV_LORA:], cos, sin)
    ckv_ref[0] = c_kv
    kr_ref[0] = k_rope.T[:C_ROPE, :]
    kcat_ref[0, :, :C_KV_LORA] = c_kv.astype(BF16)
    kcat_ref[0, :, C_KV_LORA:] = k_rope.astype(BF16)
    for hd in range(C_HEADS):
        cols = slice(hd * 128, (hd + 1) * 128)
        qcat_ref[0, hd, :, :C_KV_LORA] = _dot(q_nope[:, cols].astype(BF16), wukt_ref[hd]).astype(BF16)
        qcat_ref[0, hd, :, C_KV_LORA:] = _rope_group(q_rope[:, cols], cos, sin).astype(BF16)


def _mla_prep(x, pos, g, wdq, gq, wuqn, wuqr, wdkv, gkv, wukt, tm):
    b, s, d = x.shape
    cos, sin = _rope_tables(pos, C_ROPE, 128)
    tab = pl.BlockSpec((tm, 128), lambda i, j: (j, 0))
    return pl.pallas_call(
        _mla_prep_kernel,
        grid=(b, s // tm),
        in_specs=[pl.BlockSpec((1, tm, d), lambda i, j: (i, j, 0)),
                  _resident((1, d)), _resident(wdq.shape), _resident((1, C_Q_LORA)),
                  _resident(wuqn.shape), _resident(wuqr.shape), _resident(wdkv.shape),
                  _resident((1, C_KV_LORA)), _resident(wukt.shape), tab, tab],
        out_specs=[pl.BlockSpec((1, C_HEADS, tm, C_CAT), lambda i, j: (i, 0, j, 0)),
                   pl.BlockSpec((1, tm, C_CAT), lambda i, j: (i, j, 0)),
                   pl.BlockSpec((1, tm, C_KV_LORA), lambda i, j: (i, j, 0)),
                   pl.BlockSpec((1, C_ROPE, tm), lambda i, j: (i, 0, j))],
        out_shape=[jax.ShapeDtypeStruct((b, C_HEADS, s, C_CAT), BF16),
                   jax.ShapeDtypeStruct((b, s, C_CAT), BF16),
                   jax.ShapeDtypeStruct((b, s, C_KV_LORA), F32),
                   jax.ShapeDtypeStruct((b, C_ROPE, s), F32)],
        compiler_params=_params("parallel", "parallel"),
        name="mla_prep",
    )(x, g.reshape(1, d), wdq, gq.reshape(1, -1), wuqn, wuqr, wdkv, gkv.reshape(1, -1), wukt, cos, sin)


def _mla_out(o_lat_heads, wuv_ref, wo_ref, x):
    o = jnp.concatenate([_dot(o_lat_heads[hd].astype(BF16), wuv_ref[hd]) for hd in range(C_HEADS)], axis=1)
    return x + _dot(o.astype(BF16), wo_ref[...])


def _mla_flash_kernel(q_ref, k_ref, x_ref, wuv_ref, wo_ref, y_ref, m_ref, l_ref, acc_ref, *, tq, tk):
    i = pl.program_id(1)
    last = (i * tq + tq - 1) // tk
    c_exp = MLA_SCALE * 1.4426950408889634
    rows = 512
    group = rows // tq

    m_ref[...] = jnp.full(m_ref.shape, -jnp.inf, F32)
    l_ref[...] = jnp.zeros(l_ref.shape, F32)
    acc_ref[...] = jnp.zeros(acc_ref.shape, F32)

    def key_block(j, masked, width=tk):
        k = k_ref[0, pl.ds(pl.multiple_of(j * tk, tk), width), :]
        n_chunks = width // 128
        if masked:
            qpos = i * tq + lax.broadcasted_iota(jnp.int32, (rows, 128), 0) % tq
            lane = lax.broadcasted_iota(jnp.int32, (rows, 128), 1)
        for g in range(C_HEADS // group):
            hs = slice(g * group, (g + 1) * group)
            s = _dot_nt(q_ref[0, hs].reshape(rows, C_CAT), k)
            chunks = [s[:, c * 128:(c + 1) * 128] for c in range(n_chunks)]
            if masked:
                chunks = [jnp.where(j * tk + c * 128 + lane <= qpos, sc, -jnp.inf)
                          for c, sc in enumerate(chunks)]
            part = functools.reduce(jnp.maximum, chunks)
            m_old = m_ref[hs].reshape(rows, 128)
            m_new = jnp.maximum(m_old, jnp.max(part, axis=-1, keepdims=True))
            alpha = jnp.exp2((m_old - m_new) * c_exp)
            p = [jnp.exp2((sc - m_new) * c_exp) for sc in chunks]
            row_sum = jnp.sum(functools.reduce(jnp.add, p), axis=-1, keepdims=True)
            l_ref[hs] = (alpha * l_ref[hs].reshape(rows, 128) + row_sum).reshape(group, tq, 128)
            m_ref[hs] = m_new.reshape(group, tq, 128)
            pv = _dot(jnp.concatenate(p, axis=1).astype(BF16), k[:, :C_KV_LORA])
            acc = jnp.concatenate([alpha] * (C_KV_LORA // 128), axis=1) * acc_ref[hs].reshape(rows, C_KV_LORA)
            acc_ref[hs] = (acc + pv).reshape(group, tq, C_KV_LORA)

    def body(j, carry):
        key_block(j, masked=False)
        return carry

    lax.fori_loop(0, last, body, 0)
    ratio = tk // tq
    for r in range(ratio):
        @pl.when(i % ratio == r)
        def _(r=r):
            key_block(last, masked=True, width=(r + 1) * tq)

    heads = []
    for hd in range(C_HEADS):
        inv = 1.0 / l_ref[hd]
        heads.append(acc_ref[hd] * jnp.concatenate([inv] * (C_KV_LORA // 128), axis=1))
    y_ref[0] = _mla_out(heads, wuv_ref, wo_ref, x_ref[0])


def _mla_flash(qcat, kcat, x, wuv, wo, tq, tk):
    b, s, d = x.shape
    return pl.pallas_call(
        functools.partial(_mla_flash_kernel, tq=tq, tk=tk),
        grid=(b, s // tq),
        in_specs=[pl.BlockSpec((1, C_HEADS, tq, C_CAT), lambda i, j: (i, 0, j, 0)),
                  pl.BlockSpec((1, s, C_CAT), lambda i, j: (i, 0, 0)),
                  pl.BlockSpec((1, tq, d), lambda i, j: (i, j, 0)),
                  _resident(wuv.shape), _resident(wo.shape)],
        out_specs=pl.BlockSpec((1, tq, d), lambda i, j: (i, j, 0)),
        out_shape=jax.ShapeDtypeStruct((b, s, d), F32),
        scratch_shapes=[pltpu.VMEM((C_HEADS, tq, 128), F32), pltpu.VMEM((C_HEADS, tq, 128), F32),
                        pltpu.VMEM((C_HEADS, tq, C_KV_LORA), F32)],
        compiler_params=_params("parallel", "parallel"),
        name="mla_flash",
    )(qcat, kcat, x, wuv, wo)


def _mla_dec_kernel(pt_ref, q_ref, kn_ref, cpool_ref, rpool_ref, o_ref, cbuf, rbuf, sem,
                    *, n_pages, t_new, n_split):
    b = pl.program_id(0)
    slot = b % 2
    per_split = n_pages // n_split

    def page_copies(batch, sl, pg):
        pid = pt_ref[pg, batch]
        return (pltpu.make_async_copy(cpool_ref.at[pid], cbuf.at[sl, pg], sem.at[0, sl]),
                pltpu.make_async_copy(rpool_ref.at[pid], rbuf.at[sl, :, pg * PAGE:(pg + 1) * PAGE],
                                      sem.at[1, sl]))

    def start_batch(batch, sl):
        for pg in range(n_pages):
            for cp in page_copies(batch, sl, pg):
                cp.start()

    @pl.when(b == 0)
    def _():
        start_batch(0, 0)

    @pl.when(b + 1 < pl.num_programs(0))
    def _():
        start_batch(b + 1, 1 - slot)

    q = q_ref[0]
    rows = q.shape[0]
    q_lat = q[:, :C_KV_LORA]
    q_rope = q[:, C_KV_LORA:C_KV_LORA + C_ROPE]

    for pg in range(n_pages):
        for cp in page_copies(b, slot, pg):
            cp.wait()

    parts = []
    for part in range(n_split):
        keys = per_split * PAGE
        c_part = cbuf[slot, part * per_split:(part + 1) * per_split].reshape(keys, C_KV_LORA).astype(BF16)
        r_part = rbuf[slot, :, part * keys:(part + 1) * keys].astype(BF16)
        s = (_dot_nt(q_lat, c_part) + _dot(q_rope, r_part)) * MLA_SCALE
        m = jnp.max(s, axis=-1, keepdims=True)
        p = jnp.exp(s - m)
        parts.append((m, jnp.sum(p, axis=-1, keepdims=True), _dot(p.astype(BF16), c_part)))

    qf = q.astype(F32)
    knf = kn_ref[0].astype(F32)
    tok = lax.broadcasted_iota(jnp.int32, (rows, 1), 0) % t_new
    s_n = []
    for t in range(t_new):
        sc = jnp.sum(qf * knf[t:t + 1, :], axis=-1, keepdims=True) * MLA_SCALE
        s_n.append(jnp.where(tok >= t, sc, -jnp.inf))

    m = functools.reduce(jnp.maximum, [pm for pm, _, _ in parts] + s_n)
    l = jnp.zeros((rows, 1), F32)
    acc = jnp.zeros((rows, C_KV_LORA), F32)
    for part_m, part_l, part_acc in parts:
        w = jnp.exp(part_m - m)
        l = l + w * part_l
        acc = acc + w * part_acc
    for t in range(t_new):
        e = jnp.exp(s_n[t] - m)
        l = l + e
        acc = acc + e * knf[t:t + 1, :C_KV_LORA]
    o_ref[0] = (acc / l).astype(o_ref.dtype)


def _mla_decode(q_rows, k_new, c_pool, r_pool_t, page_table_t):
    b, rows, _ = q_rows.shape
    t_new = k_new.shape[1]
    n_pages = page_table_t.shape[0]
    n_split = 4 if n_pages % 4 == 0 else 1
    return pl.pallas_call(
        functools.partial(_mla_dec_kernel, n_pages=n_pages, t_new=t_new, n_split=n_split),
        grid_spec=pltpu.PrefetchScalarGridSpec(
            num_scalar_prefetch=1,
            grid=(b,),
            in_specs=[pl.BlockSpec((1, rows, C_CAT), lambda i, *_: (i, 0, 0)),
                      pl.BlockSpec((1, t_new, C_CAT), lambda i, *_: (i, 0, 0)),
                      pl.BlockSpec(memory_space=pl.ANY),
                      pl.BlockSpec(memory_space=pl.ANY)],
            out_specs=pl.BlockSpec((1, rows, C_KV_LORA), lambda i, *_: (i, 0, 0)),
            scratch_shapes=[pltpu.VMEM((2, n_pages, PAGE, C_KV_LORA), F32),
                            pltpu.VMEM((2, C_ROPE, n_pages * PAGE), F32),
                            pltpu.SemaphoreType.DMA((2, 2))]),
        out_shape=jax.ShapeDtypeStruct((b, rows, C_KV_LORA), BF16),
        compiler_params=_params("arbitrary"),
        name="mla_decode",
    )(page_table_t, q_rows, k_new, c_pool, r_pool_t)


def _mla_out_kernel(o_ref, wuv_ref, wo_ref, x_ref, y_ref):
    y_ref[...] = _mla_out([o_ref[hd] for hd in range(C_HEADS)], wuv_ref, wo_ref, x_ref[...])


def _mla_out_proj(o_heads, wuv, wo, x):
    m, d = x.shape
    return pl.pallas_call(
        _mla_out_kernel,
        grid=(1,),
        in_specs=[_resident(o_heads.shape), _resident(wuv.shape), _resident(wo.shape), _resident(x.shape)],
        out_specs=pl.BlockSpec((m, d), lambda i: (0, 0)),
        out_shape=jax.ShapeDtypeStruct((m, d), F32),
        compiler_params=_params("arbitrary"),
        name="mla_out_proj",
    )(o_heads, wuv, wo, x)


def _ffn_kernel(*refs, tm, tiles_per_seq, seq_in_tile, final_norm):
    x_ref, g_ref, wup_ref, wconv_ref, bconv_ref, wdown_ref = refs[:6]
    refs = refs[6:]
    if seq_in_tile:
        e1_ref, e2_ref = refs[:2]
        refs = refs[2:]
    if final_norm:
        gfin_ref = refs[0]
        refs = refs[1:]
    y_ref, gate_out_ref, gbuf = refs
    halo = 8

    if seq_in_tile:
        gbuf[0:halo, :] = jnp.zeros((halo, D_FF), F32)
    else:
        @pl.when(pl.program_id(0) % tiles_per_seq == 0)
        def _():
            gbuf[0:halo, :] = jnp.zeros((halo, D_FF), F32)

    x = x_ref[...]
    h = _rms(x, g_ref[...]).astype(BF16)
    gate = _dot(h, wup_ref[:, :D_FF])
    up = _dot(h, wup_ref[:, D_FF:])
    gbuf[halo:halo + tm, :] = gate
    tap0 = gbuf[halo - 2:halo - 2 + tm, :]
    tap1 = gbuf[halo - 1:halo - 1 + tm, :]
    if seq_in_tile:
        tok = lax.broadcasted_iota(jnp.int32, (tm, 1), 0) % seq_in_tile
        tap0 = jnp.where(tok < 2, e2_ref[...], tap0)
        tap1 = jnp.where(tok < 1, e1_ref[...], tap1)
        gate_out_ref[...] = gate
    else:
        gate_out_ref[0] = gate[tm - halo:, :]
        gbuf[0:halo, :] = gate[tm - halo:, :]
    conv = tap0 * wconv_ref[0:1, :] + bconv_ref[...] + tap1 * wconv_ref[1:2, :] + gate * wconv_ref[2:3, :]
    act = (jax.nn.silu(conv) * up).astype(BF16)
    y = x + _dot(act, wdown_ref[...])
    if final_norm:
        y = _rms(y, gfin_ref[...])
    y_ref[...] = y


def _conv_ffn(x, g, wup, wconv, bconv, wdown, *, tm, seq, conv_state=None, g_final=None):
    m, d = x.shape
    seq_in_tile = 0 if conv_state is None else seq
    row = pl.BlockSpec((tm, d), lambda i: (i, 0))
    wide = pl.BlockSpec((tm, D_FF), lambda i: (i, 0))
    args = [x, g.reshape(1, d), wup, wconv, bconv.reshape(1, D_FF), wdown]
    in_specs = [row, _resident((1, d)), _resident(wup.shape), _resident(wconv.shape),
                _resident((1, D_FF)), _resident(wdown.shape)]
    if conv_state is not None:
        assert tm % seq == 0 and seq >= 2
        zeros = jnp.zeros((m // seq, seq - 2, D_FF), F32)
        e2 = jnp.concatenate([conv_state, zeros], axis=1).reshape(m, D_FF)
        e1 = jnp.concatenate([conv_state[:, 1:], zeros, zeros[:, :1]], axis=1).reshape(m, D_FF)
        args += [e1, e2]
        in_specs += [wide, wide]
        gate_spec = wide
        gate_shape = jax.ShapeDtypeStruct((m, D_FF), F32)
    else:
        assert seq % tm == 0
        gate_spec = pl.BlockSpec((1, 8, D_FF), lambda i: (i // (seq // tm), 0, 0))
        gate_shape = jax.ShapeDtypeStruct((m // seq, 8, D_FF), F32)
    if g_final is not None:
        args.append(g_final.reshape(1, d))
        in_specs.append(_resident((1, d)))
    return pl.pallas_call(
        functools.partial(_ffn_kernel, tm=tm, tiles_per_seq=max(seq // tm, 1), seq_in_tile=seq_in_tile,
                          final_norm=g_final is not None),
        grid=(m // tm,),
        in_specs=in_specs,
        out_specs=[row, gate_spec],
        out_shape=[jax.ShapeDtypeStruct((m, d), F32), gate_shape],
        scratch_shapes=[pltpu.VMEM((tm + 8, D_FF), F32)],
        compiler_params=_params("arbitrary"),
        name="conv_ffn",
    )(*args)


def _prepare_weights(w_in_ab, w_out_ab, w_dq, w_uq, w_dkv, w_uk, w_uv, w_o_c, w_up, w_down):
    per_head = C_NOPE + C_ROPE
    uq = w_uq[0].reshape(C_Q_LORA, C_HEADS, per_head)
    wuqr = jnp.pad(uq[:, :, C_NOPE:], ((0, 0), (0, 0), (0, 128 - C_ROPE)))
    return dict(
        w_in=w_in_ab[0].astype(BF16),
        w_out=w_out_ab[0].astype(BF16),
        wdq=w_dq[0].astype(BF16),
        wuqn=uq[:, :, :C_NOPE].reshape(C_Q_LORA, C_HEADS * C_NOPE).astype(BF16),
        wuqr=wuqr.reshape(C_Q_LORA, C_HEADS * 128).astype(BF16),
        wdkv=jnp.pad(w_dkv[0], ((0, 0), (0, 128 - C_ROPE))).astype(BF16),
        wukt=w_uk[0].transpose(1, 2, 0).astype(BF16),
        wuv=w_uv[0].transpose(1, 0, 2).astype(BF16),
        wo=w_o_c[0].astype(BF16),
        w_up=[w_up[layer].astype(BF16) for layer in range(w_up.shape[0])],
        w_down=[w_down[layer].astype(BF16) for layer in range(w_down.shape[0])],
    )


def _alibi_slopes():
    return 2.0 ** (-8.0 * (jnp.arange(A_HEADS, dtype=F32) + 1.0) / A_HEADS)


AB_WIDTHS = (A_WIDTH,) * 3 + (B_WIDTH,) * 4


def _prompt_trunk(x, w, slopes, g_mix, g_ffn, g_final, g_ret, g_q, g_kv, w_conv, b_conv):
    b, s, d = x.shape
    m = b * s
    pos = jnp.arange(s)
    qa, ka, va, qb, kb, vb, gb, ka_t, va_t = _norm_proj(x, g_mix[0], w["w_in"], AB_WIDTHS, tm=512,
                                                        transposed=(1, 2))
    x = x.reshape(m, d)
    o_a = _attn_a_prompt(qa, ka, va, slopes)
    o_b, ret = _retention(qb, kb, vb, gb, pos, g_ret[0], jnp.zeros((b, B_HEADS, B_DIM, B_DIM), F32), rows=512)
    x = _out_proj(o_a.reshape(m, -1), o_b.reshape(m, -1), w["w_out"], x, tm=min(1024, m))
    tm_ffn = 256
    x, tails0 = _conv_ffn(x, g_ffn[0], w["w_up"][0], w_conv[0], b_conv[0], w["w_down"][0], tm=tm_ffn, seq=s)
    qcat, kcat, c_kv, k_rope_t = _mla_prep(x.reshape(b, s, d), pos, g_mix[1], w["wdq"], g_q[0], w["wuqn"],
                                           w["wuqr"], w["wdkv"], g_kv[0], w["wukt"], tm=512)
    x = _mla_flash(qcat, kcat, x.reshape(b, s, d), w["wuv"], w["wo"], tq=256, tk=512).reshape(m, d)
    y, tails1 = _conv_ffn(x, g_ffn[1], w["w_up"][1], w_conv[1], b_conv[1], w["w_down"][1], tm=tm_ffn, seq=s,
                          g_final=g_final)
    new_conv = jnp.stack([tails0[:, 6:8], tails1[:, 6:8]])
    keep = min(A_PATTERNS[-1][0], s)
    heads = lambda a: a[:, :, s - keep:].reshape(b, A_HEADS, A_HEAD_DIM, keep).transpose(0, 3, 1, 2)[None]
    return (y.reshape(b, s, d), heads(ka_t), heads(va_t), ret[None], c_kv[None],
            k_rope_t.transpose(0, 2, 1)[None], new_conv)


def _sample_trunk(x, w, slopes, g_mix, g_ffn, g_final, g_ret, g_q, g_kv, w_conv, b_conv,
                  cache_k, cache_v, state_ret, conv_state, c_pool, r_pool, page_table):
    b, t, d = x.shape
    m = b * t
    pos = PAST_LEN + jnp.arange(t)
    x = x.reshape(m, d)
    qa, ka, va, qb, kb, vb, gb = [a.reshape(b, t, -1) for a in
                                  _norm_proj(x.reshape(1, m, d), g_mix[0], w["w_in"], AB_WIDTHS, tm=m)]
    nbuf = cache_k.shape[1]
    feature_major = lambda a: a.transpose(0, 2, 3, 1).reshape(b, A_WIDTH, nbuf)
    o_a, new_k, new_v = _attn_a_decode(qa, ka, va, feature_major(cache_k), feature_major(cache_v), slopes)
    o_b, ret = _retention_decode(qb, kb, vb, gb, pos, g_ret[0], state_ret, n_seq=4)
    x = _out_proj(o_a.reshape(m, -1), o_b.reshape(m, -1), w["w_out"], x, tm=m)
    x, gate0 = _conv_ffn(x, g_ffn[0], w["w_up"][0], w_conv[0], b_conv[0], w["w_down"][0], tm=m, seq=t,
                         conv_state=conv_state[0])
    qcat, kcat, c_kv, k_rope_t = _mla_prep(x.reshape(1, m, d), jnp.tile(pos, b), g_mix[1], w["wdq"], g_q[0],
                                           w["wuqn"], w["wuqr"], w["wdkv"], g_kv[0], w["wukt"], tm=m)
    q_rows = qcat.reshape(C_HEADS, b, t, C_CAT).transpose(1, 0, 2, 3).reshape(b, C_HEADS * t, C_CAT)
    o_lat = _mla_decode(q_rows, kcat.reshape(b, t, C_CAT), c_pool, r_pool.transpose(0, 2, 1), page_table.T)
    o_heads = o_lat.reshape(b, C_HEADS, t, C_KV_LORA).transpose(1, 0, 2, 3).reshape(C_HEADS, m, C_KV_LORA)
    x = _mla_out_proj(o_heads, w["wuv"], w["wo"], x)
    y, gate1 = _conv_ffn(x, g_ffn[1], w["w_up"][1], w_conv[1], b_conv[1], w["w_down"][1], tm=m, seq=t,
                         conv_state=conv_state[1], g_final=g_final)
    new_conv = jnp.stack([gate0.reshape(b, t, D_FF)[:, t - 2:], gate1.reshape(b, t, D_FF)[:, t - 2:]])
    heads = lambda a: a.reshape(b, A_HEADS, A_HEAD_DIM, nbuf).transpose(0, 3, 1, 2)[None]
    return (y.reshape(b, t, d), heads(new_k), heads(new_v), ret[None], c_kv.reshape(1, b, t, C_KV_LORA),
            k_rope_t.reshape(C_ROPE, b, t).transpose(1, 2, 0)[None], new_conv)


def kernel(x_prompt, x_sample, cache_a_k, cache_a_v, state_ret, cache_c_kv, cache_k_rope, state_ffn_conv,
           page_table, g_mix, g_ffn, g_final, w_in_ab, w_out_ab, g_ret, w_dq, g_q, w_uq, w_dkv, g_kv,
           w_uk, w_uv, w_o_c, w_up, w_conv, b_conv, w_down):
    assert g_mix.shape[0] == 2 and w_in_ab.shape[0] == 1 and w_dq.shape[0] == 1
    w = _prepare_weights(w_in_ab, w_out_ab, w_dq, w_uq, w_dkv, w_uk, w_uv, w_o_c, w_up, w_down)
    shared = (w, _alibi_slopes(), g_mix, g_ffn, g_final, g_ret, g_q, g_kv, w_conv, b_conv)
    out_p = _prompt_trunk(x_prompt, *shared)
    out_s = _sample_trunk(x_sample, *shared, cache_a_k[0], cache_a_v[0], state_ret[0], state_ffn_conv,
                          cache_c_kv[0], cache_k_rope[0], page_table)
    return (out_p[0], out_s[0]) + out_p[1:] + out_s[1:]
```

```python
import functools

import jax
import jax.numpy as jnp
from jax import lax
from jax.experimental import pallas as pl
from jax.experimental.pallas import tpu as pltpu

F32 = jnp.float32
BF16 = jnp.bfloat16

D_MODEL = 1024
A_HEADS = 8
A_HEAD_DIM = 64
A_WIDTH = A_HEADS * A_HEAD_DIM
A_PATTERNS = ((128, 1), (512, 4), (2048, 16))
BAND = 128
B_HEADS = 4
B_DIM = 128
B_WIDTH = B_HEADS * B_DIM
RET_CHUNK = 128
C_HEADS = 8
C_Q_LORA = 384
C_KV_LORA = 256
C_NOPE = 128
C_ROPE = 64
C_V_DIM = 128
C_CAT = C_KV_LORA + 128
MLA_SCALE = (C_NOPE + C_ROPE) ** -0.5
D_FF = 2816
ROPE_BASE = 10000.0
EPS = 1e-6
PAST_LEN = 8192
PAGE = 128

VMEM_LIMIT = 56 * 1024 * 1024

NT_DIMS = (((1,), (1,)), ((), ()))
TN_DIMS = (((0,), (0,)), ((), ()))


def _params(*sem):
    return pltpu.CompilerParams(dimension_semantics=sem, vmem_limit_bytes=VMEM_LIMIT)


def _resident(shape):
    nd = len(shape)
    return pl.BlockSpec(shape, lambda *_: (0,) * nd, pipeline_mode=pl.Buffered(1))


def _rms(x, g):
    return x * lax.rsqrt(jnp.mean(x * x, axis=-1, keepdims=True) + EPS) * g


def _dot(a, b):
    return jnp.dot(a, b, preferred_element_type=F32)


def _dot_nt(a, b):
    return lax.dot_general(a, b, NT_DIMS, preferred_element_type=F32)


def _norm_proj_kernel(x_ref, g_ref, w_ref, *out_refs, n_groups, transposed):
    h = _rms(x_ref[0], g_ref[...]).astype(BF16)
    t_refs = dict(zip(transposed, out_refs[n_groups:]))
    off = 0
    for grp, o_ref in enumerate(out_refs[:n_groups]):
        n = o_ref.shape[-1]
        r = _dot(h, w_ref[:, off:off + n])
        o_ref[0] = r
        if grp in t_refs:
            t_refs[grp][0] = r.T
        off += n


def _norm_proj(x, g, w, widths, tm, transposed=()):
    b, s, d = x.shape
    return pl.pallas_call(
        functools.partial(_norm_proj_kernel, n_groups=len(widths), transposed=tuple(transposed)),
        grid=(b, s // tm),
        in_specs=[pl.BlockSpec((1, tm, d), lambda i, j: (i, j, 0)),
                  _resident((1, d)),
                  _resident(w.shape)],
        out_specs=([pl.BlockSpec((1, tm, n), lambda i, j: (i, j, 0)) for n in widths]
                   + [pl.BlockSpec((1, widths[t], tm), lambda i, j: (i, 0, j)) for t in transposed]),
        out_shape=([jax.ShapeDtypeStruct((b, s, n), F32) for n in widths]
                   + [jax.ShapeDtypeStruct((b, widths[t], s), F32) for t in transposed]),
        compiler_params=_params("parallel", "parallel"),
        name="norm_proj",
    )(x, g.reshape(1, d), w)


def _unrolled_loop(n_items, fn):
    unroll = next(u for u in (4, 3, 2, 1) if n_items % u == 0)
    if n_items <= unroll:
        for t in range(n_items):
            fn(t)
        return

    def body(it, carry):
        for u in range(unroll):
            fn(it * unroll + u)
        return carry

    lax.fori_loop(0, n_items // unroll, body, 0)


def _attn_a_kernel(slopes_ref, q_ref, k_ref, v_ref, o_ref, os_ref, ls_ref, bias_ref, *, seq):
    pair = pl.program_id(1)
    is_a = lax.broadcasted_iota(jnp.int32, (1, 128), 1) < A_HEAD_DIM

    row = lax.broadcasted_iota(jnp.int32, (2 * BAND, 2 * BAND), 0)
    col = lax.broadcasted_iota(jnp.int32, (2 * BAND, 2 * BAND), 1)
    rel = BAND + (row & (BAND - 1)) - col
    slope = jnp.where(row < BAND, slopes_ref[2 * pair], slopes_ref[2 * pair + 1])
    in_band = (rel >= 0) & (rel <= BAND)
    for pat, (_, dil) in enumerate(A_PATTERNS):
        bias_ref[pat] = jnp.where(in_band, -slope * (rel * dil).astype(F32), -jnp.inf)

    def block(pat, dil, q_start, k_start, first):
        nk = BAND if first else 2 * BAND
        q = q_ref[0, pl.ds(q_start, BAND, stride=dil), :] * (A_HEAD_DIM ** -0.5)
        k = k_ref[0, pl.ds(k_start, nk, stride=dil), :].astype(BF16)
        v = v_ref[0, pl.ds(k_start, nk, stride=dil), :].astype(BF16)
        q2 = jnp.concatenate([jnp.where(is_a, q, 0.0), jnp.where(is_a, 0.0, q)], axis=0).astype(BF16)
        s = _dot_nt(q2, k) + (bias_ref[pat, :, BAND:] if first else bias_ref[pat])
        m = jnp.max(s, axis=-1, keepdims=True)
        e = jnp.exp(s - m)
        l = jnp.sum(e, axis=-1, keepdims=True)
        o = _dot(e.astype(BF16), v) / l
        lse = m + jnp.log(l)
        os_ref[pat, pl.ds(q_start, BAND, stride=dil), :] = jnp.where(is_a, o[:BAND], o[BAND:])
        ls_ref[pat, pl.ds(q_start, BAND, stride=dil), :] = jnp.where(is_a, lse[:BAND], lse[BAND:])

    for pat, (_, dil) in enumerate(A_PATTERNS):
        nb = seq // dil // BAND

        def first_block(r, pat=pat, dil=dil):
            block(pat, dil, r, r, True)

        def later_block(t, pat=pat, dil=dil, nb=nb):
            r = t // (nb - 1)
            j = 1 + t % (nb - 1)
            block(pat, dil, r + dil * BAND * j, r + dil * BAND * (j - 1), False)

        _unrolled_loop(dil, first_block)
        if nb > 1:
            _unrolled_loop(dil * (nb - 1), later_block)

    rows = 256

    def combine(i, carry):
        sl = pl.ds(pl.multiple_of(i * rows, rows), rows)
        l0, l1, l2 = ls_ref[0, sl, :], ls_ref[1, sl, :], ls_ref[2, sl, :]
        mx = jnp.maximum(jnp.maximum(l0, l1), l2)
        w0, w1, w2 = jnp.exp(l0 - mx), jnp.exp(l1 - mx), jnp.exp(l2 - mx)
        num = w0 * os_ref[0, sl, :] + w1 * os_ref[1, sl, :] + w2 * os_ref[2, sl, :]
        o_ref[0, sl, :] = (num / (w0 + w1 + w2)).astype(o_ref.dtype)
        return carry

    lax.fori_loop(0, seq // rows, combine, 0)


def _attn_a_prompt(q, k, v, slopes):
    b, s, _ = q.shape
    assert s % (A_PATTERNS[-1][1] * BAND) == 0
    spec = pl.BlockSpec((1, s, 128), lambda i, p, *_: (i, 0, p))
    return pl.pallas_call(
        functools.partial(_attn_a_kernel, seq=s),
        grid_spec=pltpu.PrefetchScalarGridSpec(
            num_scalar_prefetch=1,
            grid=(b, A_WIDTH // 128),
            in_specs=[spec, spec, spec],
            out_specs=spec,
            scratch_shapes=[pltpu.VMEM((3, s, 128), F32), pltpu.VMEM((3, s, 128), F32),
                            pltpu.VMEM((3, 2 * BAND, 2 * BAND), F32)]),
        out_shape=jax.ShapeDtypeStruct((b, s, A_WIDTH), BF16),
        compiler_params=_params("parallel", "parallel"),
        name="attn_a_prompt",
    )(slopes, q, k, v)


def _pattern_multiplicity(delta):
    mult = jnp.zeros(delta.shape, F32)
    for window, dil in A_PATTERNS:
        hit = (delta >= 0) & (delta <= window)
        if dil > 1:
            hit = hit & ((delta & (dil - 1)) == 0)
        mult = mult + jnp.where(hit, 1.0, 0.0)
    return mult


def _attn_a_dec_body(slopes_ref, half, seq_in_group, q_ref, kn_ref, vn_ref, knt_ref, vnt_ref, ck_ref, cv_ref,
                     o_ref, nk_ref, nv_ref, *, nbuf, t_new):
    width = q_ref.shape[-1]
    heads = width // A_HEAD_DIM
    rows = heads * t_new
    q = q_ref[0]
    q_rows = jnp.concatenate([q] * heads, axis=0)
    row = lax.broadcasted_iota(jnp.int32, (rows, width), 0)
    lane = lax.broadcasted_iota(jnp.int32, (rows, width), 1)
    qm = jnp.where(row // t_new == lane // A_HEAD_DIM, q_rows, 0.0).astype(BF16)

    rcol = lax.broadcasted_iota(jnp.int32, (rows, 1), 0)
    slope = jnp.zeros((rows, 1), F32)
    for h in range(heads):
        slope = jnp.where(rcol // t_new == h, slopes_ref[half * heads + h], slope)
    tok = rcol % t_new

    s_c = _dot(qm, ck_ref[0].astype(BF16)) * (A_HEAD_DIM ** -0.5)
    pos = lax.broadcasted_iota(jnp.int32, (rows, nbuf), 1)
    delta_c = nbuf + tok - pos
    mult_c = _pattern_multiplicity(delta_c)
    s_c = jnp.where(mult_c > 0, s_c - slope * delta_c.astype(F32), -jnp.inf)
    qf = qm.astype(F32)
    knf = kn_ref[0].astype(BF16).astype(F32)
    vnf = vn_ref[0].astype(BF16).astype(F32)
    s_n, mult_n = [], []
    for t in range(t_new):
        delta = tok - t
        mult = _pattern_multiplicity(delta)
        sc = jnp.sum(qf * knf[t:t + 1, :], axis=-1, keepdims=True) * (A_HEAD_DIM ** -0.5)
        s_n.append(jnp.where(mult > 0, sc - slope * delta.astype(F32), -jnp.inf))
        mult_n.append(mult)
    m = jnp.max(s_c, axis=-1, keepdims=True)
    for sc in s_n:
        m = jnp.maximum(m, sc)
    e_c = jnp.exp(s_c - m) * mult_c
    l = jnp.sum(e_c, axis=-1, keepdims=True)
    acc = _dot_nt(e_c.astype(BF16), cv_ref[0].astype(BF16))
    for t in range(t_new):
        e = jnp.exp(s_n[t] - m) * mult_n[t]
        l = l + e
        acc = acc + e * vnf[t:t + 1, :]
    res = acc / l
    lane_head = lax.broadcasted_iota(jnp.int32, (t_new, width), 1) // A_HEAD_DIM
    out = jnp.zeros((t_new, width), F32)
    for h in range(heads):
        out = jnp.where(lane_head == h, res[h * t_new:(h + 1) * t_new], out)
    o_ref[0] = out.astype(o_ref.dtype)

    is_new = lax.broadcasted_iota(jnp.int32, (width, 128), 1) >= 128 - t_new
    to_tail = (128 - t_new - seq_in_group * t_new) % 128
    for c_ref, nt_ref, n_ref in ((ck_ref, knt_ref, nk_ref), (cv_ref, vnt_ref, nv_ref)):
        rolled = pltpu.roll(c_ref[0], nbuf - t_new, 1)
        n_ref[0, :, 0:nbuf - 128] = rolled[:, 0:nbuf - 128]
        n_ref[0, :, nbuf - 128:nbuf] = jnp.where(is_new, pltpu.roll(nt_ref[...], to_tail, 1),
                                                 rolled[:, nbuf - 128:nbuf])


def _retention_head(q, k, v, gate, rope_q, rope_k, dmask, qdec, kdec, cdec, g, state):
    (cos_q, sin_q), (cos_k, sin_k) = rope_q, rope_k
    qr = (q * cos_q + pltpu.roll(q, B_DIM // 2, 1) * sin_q).astype(BF16)
    kr = (k * cos_k + pltpu.roll(k, B_DIM // 2, 1) * sin_k) * (B_DIM ** -0.5)
    v = v.astype(BF16)
    att = _dot_nt(qr, kr.astype(BF16)) * dmask
    o = _dot(att.astype(BF16), v) + _dot(qr, state.astype(BF16)) * qdec
    kd = (kr * kdec).astype(BF16)
    new_state = state * cdec + lax.dot_general(kd, v, TN_DIMS, preferred_element_type=F32)
    d = o - jnp.mean(o, axis=-1, keepdims=True)
    y = d * lax.rsqrt(jnp.mean(d * d, axis=-1, keepdims=True) + EPS) * g
    return y * jax.nn.silu(gate), new_state


def _retention_kernel(q_ref, k_ref, v_ref, gate_ref, cos_ref, sin_ref, dmask_ref, qdec_ref, kdec_ref,
                      cdec_ref, g_ref, s0_ref, o_ref, s_ref, *, chunk, n_chunks):
    @pl.when(pl.program_id(1) == 0)
    def _():
        s_ref[...] = s0_ref[...]

    for c in range(n_chunks):
        rows = slice(c * chunk, (c + 1) * chunk)
        rope = (cos_ref[rows, :], sin_ref[rows, :])
        for h in range(B_HEADS):
            cols = slice(h * B_DIM, (h + 1) * B_DIM)
            y, s_ref[0, h] = _retention_head(
                q_ref[0, rows, cols], k_ref[0, rows, cols], v_ref[0, rows, cols], gate_ref[0, rows, cols],
                rope, rope, dmask_ref[h], qdec_ref[h], kdec_ref[h], cdec_ref[h], g_ref[:, cols], s_ref[0, h])
            o_ref[0, rows, cols] = y.astype(o_ref.dtype)


def _retention_dec_kernel(q_ref, k_ref, v_ref, gate_ref, cosq_ref, sinq_ref, cosk_ref, sink_ref, dmask_ref,
                          qdec_ref, kdec_ref, cdec_ref, g_ref, s0_ref, o_ref, s_ref, kpad, vpad, *, n_seq):
    t_rows = k_ref.shape[1]
    kpad[...] = jnp.zeros(kpad.shape, F32)
    vpad[...] = jnp.zeros(vpad.shape, F32)
    rope_q = (cosq_ref[...], sinq_ref[...])
    rope_k = (cosk_ref[...], sink_ref[...])
    for b in range(n_seq):
        kpad[b, 0:t_rows, :] = k_ref[b]
        vpad[b, 0:t_rows, :] = v_ref[b]
        for h in range(B_HEADS):
            cols = slice(h * B_DIM, (h + 1) * B_DIM)
            y, s_ref[b, h] = _retention_head(
                q_ref[b, :, cols], kpad[b, :, cols], vpad[b, :, cols], gate_ref[b, :, cols],
                rope_q, rope_k, dmask_ref[h], qdec_ref[h], kdec_ref[h], cdec_ref[h], g_ref[:, cols], s0_ref[b, h])
            o_ref[b, :, cols] = y.astype(o_ref.dtype)


def _retention_tables(chunk, q_pad, k_pad):
    lg = jnp.log(1.0 - 2.0 ** (-5.0 - jnp.arange(B_HEADS, dtype=F32)))
    qi = jnp.arange(q_pad, dtype=F32)
    ki = jnp.arange(k_pad, dtype=F32)
    rel = qi[:, None] - ki[None, :]
    live = (rel >= 0) & (qi < chunk)[:, None] & (ki < chunk)[None, :]
    dmask = jnp.where(live, jnp.exp(jnp.maximum(rel, 0.0)[None] * lg[:, None, None]), 0.0)
    qdec = jnp.where((qi < chunk)[None, :], jnp.exp((qi + 1.0)[None, :] * lg[:, None]), 0.0)
    kdec = jnp.where((ki < chunk)[None, :], jnp.exp((chunk - 1.0 - ki)[None, :] * lg[:, None]), 0.0)
    cdec = jnp.exp(chunk * lg)
    bc = lambda a: jnp.broadcast_to(a[..., None], a.shape + (B_DIM,))
    return dmask, bc(qdec), bc(kdec), bc(cdec[:, None])


def _rope_tables(pos, dim, width):
    half = dim // 2
    inv = ROPE_BASE ** (-jnp.arange(half, dtype=F32) / half)
    ang = pos.astype(F32)[:, None] * inv[None, :]
    cos, sin = jnp.cos(ang), jnp.sin(ang)
    zeros = jnp.zeros((pos.shape[0], width - dim), F32)
    return (jnp.concatenate([cos, cos, zeros], axis=1), jnp.concatenate([-sin, sin, zeros], axis=1))


def _retention_decode(q, k, v, gate, pos, g_ret, s0, n_seq):
    b, t, _ = q.shape
    t_pad = 8
    assert t <= t_pad and b % n_seq == 0
    pad_rows = lambda a: jnp.pad(a, ((0, 0), (0, t_pad - t), (0, 0)))
    cosk, sink = _rope_tables(jnp.pad(pos, (0, RET_CHUNK - t)), B_DIM, B_DIM)
    dmask, qdec, kdec, cdec = _retention_tables(t, t_pad, RET_CHUNK)
    tok = pl.BlockSpec((n_seq, t_pad, B_WIDTH), lambda i: (i, 0, 0))
    st = pl.BlockSpec((n_seq, B_HEADS, B_DIM, B_DIM), lambda i: (i, 0, 0, 0))
    consts = (cosk[:t_pad], sink[:t_pad], cosk, sink, dmask, qdec, kdec, cdec, g_ret.reshape(1, B_WIDTH))
    o, s_new = pl.pallas_call(
        functools.partial(_retention_dec_kernel, n_seq=n_seq),
        grid=(b // n_seq,),
        in_specs=[tok, tok, tok, tok] + [_resident(c.shape) for c in consts] + [st],
        out_specs=[tok, st],
        out_shape=[jax.ShapeDtypeStruct((b, t_pad, B_WIDTH), BF16),
                   jax.ShapeDtypeStruct((b, B_HEADS, B_DIM, B_DIM), F32)],
        scratch_shapes=[pltpu.VMEM((n_seq, RET_CHUNK, B_WIDTH), F32),
                        pltpu.VMEM((n_seq, RET_CHUNK, B_WIDTH), F32)],
        compiler_params=_params("parallel"),
        name="retention_decode",
    )(pad_rows(q), pad_rows(k), pad_rows(v), pad_rows(gate), *consts, s0)
    return o[:, :t], s_new


def _retention(q, k, v, gate, pos, g_ret, s0, rows):
    b, s, _ = q.shape
    cos, sin = _rope_tables(pos, B_DIM, B_DIM)
    dmask, qdec, kdec, cdec = _retention_tables(RET_CHUNK, RET_CHUNK, RET_CHUNK)
    tok = pl.BlockSpec((1, rows, B_WIDTH), lambda i, j: (i, j, 0))
    tab = pl.BlockSpec((rows, B_DIM), lambda i, j: (j, 0))
    st = pl.BlockSpec((1, B_HEADS, B_DIM, B_DIM), lambda i, j: (i, 0, 0, 0))
    return pl.pallas_call(
        functools.partial(_retention_kernel, chunk=RET_CHUNK, n_chunks=rows // RET_CHUNK),
        grid=(b, s // rows),
        in_specs=[tok, tok, tok, tok, tab, tab, _resident(dmask.shape), _resident(qdec.shape),
                  _resident(kdec.shape), _resident(cdec.shape), _resident((1, B_WIDTH)), st],
        out_specs=[tok, st],
        out_shape=[jax.ShapeDtypeStruct((b, s, B_WIDTH), BF16),
                   jax.ShapeDtypeStruct((b, B_HEADS, B_DIM, B_DIM), F32)],
        compiler_params=_params("parallel", "arbitrary"),
        name="retention",
    )(q, k, v, gate, cos, sin, dmask, qdec, kdec, cdec, g_ret.reshape(1, B_WIDTH), s0)


def _out_proj_kernel(a_ref, b_ref, w_ref, x_ref, y_ref):
    na = a_ref.shape[-1]
    y_ref[...] = x_ref[...] + _dot(a_ref[...], w_ref[:na, :]) + _dot(b_ref[...], w_ref[na:, :])


def _out_proj(a, b, w, x, tm):
    m, d = x.shape
    return pl.pallas_call(
        _out_proj_kernel,
        grid=(m // tm,),
        in_specs=[pl.BlockSpec((tm, a.shape[1]), lambda i: (i, 0)),
                  pl.BlockSpec((tm, b.shape[1]), lambda i: (i, 0)),
                  _resident(w.shape),
                  pl.BlockSpec((tm, d), lambda i: (i, 0))],
        out_specs=pl.BlockSpec((tm, d), lambda i: (i, 0)),
        out_shape=jax.ShapeDtypeStruct((m, d), F32),
        compiler_params=_params("parallel"),
        name="out_proj",
    )(a, b, w, x)


def _rope_group(x, cos, sin):
    return x * cos + (pltpu.roll(x, C_ROPE // 2, 1) + pltpu.roll(x, 128 - C_ROPE // 2, 1)) * sin


def _mla_prep_kernel(x_ref, g_ref, wdq_ref, gq_ref, wuqn_ref, wuqr_ref, wdkv_ref, gkv_ref, wukt_ref,
                     cos_ref, sin_ref, qcat_ref, kcat_ref, ckv_ref, kr_ref):
    h = _rms(x_ref[0], g_ref[...]).astype(BF16)
    cq = _rms(_dot(h, wdq_ref[...]), gq_ref[...]).astype(BF16)
    q_nope = _dot(cq, wuqn_ref[...])
    q_rope = _dot(cq, wuqr_ref[...])
    kv = _dot(h, wdkv_ref[...])
    cos = cos_ref[...]
    sin = sin_ref[...]
    c_kv = _rms(kv[:, :C_KV_LORA], gkv_ref[...])
    k_rope = _rope_group(kv[:, C_KV_LORA:], cos, sin)
    ckv_ref[0] = c_kv
    kr_ref[0] = k_rope.T[:C_ROPE, :]
    kcat_ref[0, :, :C_KV_LORA] = c_kv.astype(BF16)
    kcat_ref[0, :, C_KV_LORA:] = k_rope.astype(BF16)
    for hd in range(C_HEADS):
        cols = slice(hd * 128, (hd + 1) * 128)
        qcat_ref[0, hd, :, :C_KV_LORA] = _dot(q_nope[:, cols].astype(BF16), wukt_ref[hd]).astype(BF16)
        qcat_ref[0, hd, :, C_KV_LORA:] = _rope_group(q_rope[:, cols], cos, sin).astype(BF16)


def _mla_prep(x, pos, g, wdq, gq, wuqn, wuqr, wdkv, gkv, wukt, tm):
    b, s, d = x.shape
    cos, sin = _rope_tables(pos, C_ROPE, 128)
    tab = pl.BlockSpec((tm, 128), lambda i, j: (j, 0))
    return pl.pallas_call(
        _mla_prep_kernel,
        grid=(b, s // tm),
        in_specs=[pl.BlockSpec((1, tm, d), lambda i, j: (i, j, 0)),
                  _resident((1, d)), _resident(wdq.shape), _resident((1, C_Q_LORA)),
                  _resident(wuqn.shape), _resident(wuqr.shape), _resident(wdkv.shape),
                  _resident((1, C_KV_LORA)), _resident(wukt.shape), tab, tab],
        out_specs=[pl.BlockSpec((1, C_HEADS, tm, C_CAT), lambda i, j: (i, 0, j, 0)),
                   pl.BlockSpec((1, tm, C_CAT), lambda i, j: (i, j, 0)),
                   pl.BlockSpec((1, tm, C_KV_LORA), lambda i, j: (i, j, 0)),
                   pl.BlockSpec((1, C_ROPE, tm), lambda i, j: (i, 0, j))],
        out_shape=[jax.ShapeDtypeStruct((b, C_HEADS, s, C_CAT), BF16),
                   jax.ShapeDtypeStruct((b, s, C_CAT), BF16),
                   jax.ShapeDtypeStruct((b, s, C_KV_LORA), F32),
                   jax.ShapeDtypeStruct((b, C_ROPE, s), F32)],
        compiler_params=_params("parallel", "parallel"),
        name="mla_prep",
    )(x, g.reshape(1, d), wdq, gq.reshape(1, -1), wuqn, wuqr, wdkv, gkv.reshape(1, -1), wukt, cos, sin)


def _mla_out(o_lat_heads, wuv_ref, wo_ref, x):
    o = jnp.concatenate([_dot(o_lat_heads[hd].astype(BF16), wuv_ref[hd]) for hd in range(C_HEADS)], axis=1)
    return x + _dot(o.astype(BF16), wo_ref[...])


def _mla_flash_blocks(i, q_ref, k_ref, m_ref, l_ref, acc_ref, *, tq, tk):
    last = (i * tq + tq - 1) // tk
    c_exp = MLA_SCALE * 1.4426950408889634
    rows = 512
    group = rows // tq

    m_ref[...] = jnp.full(m_ref.shape, -jnp.inf, F32)
    l_ref[...] = jnp.zeros(l_ref.shape, F32)
    acc_ref[...] = jnp.zeros(acc_ref.shape, F32)

    def key_block(j, masked, width=tk):
        k = k_ref[0, pl.ds(pl.multiple_of(j * tk, tk), width), :]
        n_chunks = width // 128
        if masked:
            qpos = i * tq + lax.broadcasted_iota(jnp.int32, (rows, 128), 0) % tq
            lane = lax.broadcasted_iota(jnp.int32, (rows, 128), 1)
        for g in range(C_HEADS // group):
            hs = slice(g * group, (g + 1) * group)
            s = _dot_nt(q_ref[0, hs].reshape(rows, C_CAT), k)
            chunks = [s[:, c * 128:(c + 1) * 128] for c in range(n_chunks)]
            if masked:
                chunks = [jnp.where(j * tk + c * 128 + lane <= qpos, sc, -jnp.inf)
                          for c, sc in enumerate(chunks)]
            part = functools.reduce(jnp.maximum, chunks)
            m_old = m_ref[hs].reshape(rows, 128)
            m_new = jnp.maximum(m_old, jnp.max(part, axis=-1, keepdims=True))
            alpha = jnp.exp2((m_old - m_new) * c_exp)
            p = [jnp.exp2((sc - m_new) * c_exp) for sc in chunks]
            row_sum = jnp.sum(functools.reduce(jnp.add, p), axis=-1, keepdims=True)
            l_ref[hs] = (alpha * l_ref[hs].reshape(rows, 128) + row_sum).reshape(group, tq, 128)
            m_ref[hs] = m_new.reshape(group, tq, 128)
            pv = _dot(jnp.concatenate(p, axis=1).astype(BF16), k[:, :C_KV_LORA])
            acc = jnp.concatenate([alpha] * (C_KV_LORA // 128), axis=1) * acc_ref[hs].reshape(rows, C_KV_LORA)
            acc_ref[hs] = (acc + pv).reshape(group, tq, C_KV_LORA)

    def body(j, carry):
        key_block(j, masked=False)
        return carry

    lax.fori_loop(0, last, body, 0)
    ratio = tk // tq
    for r in range(ratio):
        @pl.when(i % ratio == r)
        def _(r=r):
            key_block(last, masked=True, width=(r + 1) * tq)


def _mla_flash_finish(x_ref, wuv_ref, wo_ref, y_ref, l_ref, acc_ref):
    heads = []
    for hd in range(C_HEADS):
        inv = 1.0 / l_ref[hd]
        heads.append(acc_ref[hd] * jnp.concatenate([inv] * (C_KV_LORA // 128), axis=1))
    y_ref[0] = _mla_out(heads, wuv_ref, wo_ref, x_ref[0])


def _flash_decode_kernel(slopes_ref, q_ref, k_ref, x_ref, wuv_ref, wo_ref,
                         dq_ref, dkn_ref, dvn_ref, dknt_ref, dvnt_ref, dck_ref, dcv_ref,
                         y_ref, do_ref, dnk_ref, dnv_ref, m_ref, l_ref, acc_ref,
                         *, tq, tk, nbuf, t_new):
    i = pl.program_id(1)
    step = pl.program_id(0) * pl.num_programs(1) + i
    _mla_flash_blocks(i, q_ref, k_ref, m_ref, l_ref, acc_ref, tq=tq, tk=tk)
    _attn_a_dec_body(slopes_ref, step % 2, (step // 2) % (128 // t_new), dq_ref, dkn_ref, dvn_ref, dknt_ref,
                     dvnt_ref, dck_ref, dcv_ref, do_ref, dnk_ref, dnv_ref, nbuf=nbuf, t_new=t_new)
    _mla_flash_finish(x_ref, wuv_ref, wo_ref, y_ref, l_ref, acc_ref)


def _mla_flash_with_decode(qcat, kcat, x, wuv, wo, slopes, dq, dkn, dvn, dkn_t, dvn_t, cache_kt, cache_vt,
                           tq, tk):
    b, s, d = x.shape
    bs, t_new, _ = dq.shape
    nbuf = cache_kt.shape[2]
    nq = s // tq
    width = A_WIDTH // 2
    assert b * nq == 2 * bs and 128 % t_new == 0
    per_group = 128 // t_new
    seq = lambda i, j: (i * nq + j) // 2
    half = lambda i, j: (i * nq + j) % 2
    small = pl.BlockSpec((1, t_new, width), lambda i, j, *_: (seq(i, j), 0, half(i, j)))
    newt = pl.BlockSpec((width, 128), lambda i, j, *_: (half(i, j), seq(i, j) // per_group))
    big = pl.BlockSpec((1, width, nbuf), lambda i, j, *_: (seq(i, j), half(i, j), 0))
    return pl.pallas_call(
        functools.partial(_flash_decode_kernel, tq=tq, tk=tk, nbuf=nbuf, t_new=t_new),
        grid_spec=pltpu.PrefetchScalarGridSpec(
            num_scalar_prefetch=1,
            grid=(b, nq),
            in_specs=[pl.BlockSpec((1, C_HEADS, tq, C_CAT), lambda i, j, *_: (i, 0, j, 0)),
                      pl.BlockSpec((1, s, C_CAT), lambda i, j, *_: (i, 0, 0)),
                      pl.BlockSpec((1, tq, d), lambda i, j, *_: (i, j, 0)),
                      _resident(wuv.shape), _resident(wo.shape),
                      small, small, small, newt, newt, big, big],
            out_specs=[pl.BlockSpec((1, tq, d), lambda i, j, *_: (i, j, 0)), small, big, big],
            scratch_shapes=[pltpu.VMEM((C_HEADS, tq, 128), F32), pltpu.VMEM((C_HEADS, tq, 128), F32),
                            pltpu.VMEM((C_HEADS, tq, C_KV_LORA), F32)]),
        out_shape=[jax.ShapeDtypeStruct((b, s, d), F32),
                   jax.ShapeDtypeStruct((bs, t_new, A_WIDTH), BF16),
                   jax.ShapeDtypeStruct(cache_kt.shape, F32),
                   jax.ShapeDtypeStruct(cache_vt.shape, F32)],
        compiler_params=_params("parallel", "parallel"),
        name="mla_flash_with_decode",
    )(slopes, qcat, kcat, x, wuv, wo, dq, dkn, dvn, dkn_t, dvn_t, cache_kt, cache_vt)


def _mla_dec_kernel(pt_ref, q_ref, kn_ref, cpool_ref, rpool_ref, o_ref, cbuf, rbuf, sem,
                    *, n_pages, t_new, n_split):
    b = pl.program_id(0)
    slot = b % 2
    per_split = n_pages // n_split

    def page_copies(batch, sl, pg):
        pid = pt_ref[pg, batch]
        return (pltpu.make_async_copy(cpool_ref.at[pid], cbuf.at[sl, pg], sem.at[0, sl]),
                pltpu.make_async_copy(rpool_ref.at[pid], rbuf.at[sl, :, pg * PAGE:(pg + 1) * PAGE],
                                      sem.at[1, sl]))

    def start_batch(batch, sl):
        for pg in range(n_pages):
            for cp in page_copies(batch, sl, pg):
                cp.start()

    @pl.when(b == 0)
    def _():
        start_batch(0, 0)

    @pl.when(b + 1 < pl.num_programs(0))
    def _():
        start_batch(b + 1, 1 - slot)

    q = q_ref[0]
    rows = q.shape[0]
    q_lat = q[:, :C_KV_LORA]
    q_rope = q[:, C_KV_LORA:C_KV_LORA + C_ROPE]

    for pg in range(n_pages):
        for cp in page_copies(b, slot, pg):
            cp.wait()

    parts = []
    for part in range(n_split):
        keys = per_split * PAGE
        c_part = cbuf[slot, part * per_split:(part + 1) * per_split].reshape(keys, C_KV_LORA).astype(BF16)
        r_part = rbuf[slot, :, part * keys:(part + 1) * keys].astype(BF16)
        s = (_dot_nt(q_lat, c_part) + _dot(q_rope, r_part)) * MLA_SCALE
        m = jnp.max(s, axis=-1, keepdims=True)
        p = jnp.exp(s - m)
        parts.append((m, jnp.sum(p, axis=-1, keepdims=True), _dot(p.astype(BF16), c_part)))

    qf = q.astype(F32)
    knf = kn_ref[0].astype(F32)
    tok = lax.broadcasted_iota(jnp.int32, (rows, 1), 0) % t_new
    s_n = []
    for t in range(t_new):
        sc = jnp.sum(qf * knf[t:t + 1, :], axis=-1, keepdims=True) * MLA_SCALE
        s_n.append(jnp.where(tok >= t, sc, -jnp.inf))

    m = functools.reduce(jnp.maximum, [pm for pm, _, _ in parts] + s_n)
    l = jnp.zeros((rows, 1), F32)
    acc = jnp.zeros((rows, C_KV_LORA), F32)
    for part_m, part_l, part_acc in parts:
        w = jnp.exp(part_m - m)
        l = l + w * part_l
        acc = acc + w * part_acc
    for t in range(t_new):
        e = jnp.exp(s_n[t] - m)
        l = l + e
        acc = acc + e * knf[t:t + 1, :C_KV_LORA]
    o_ref[0] = (acc / l).astype(o_ref.dtype)


def _mla_decode(q_rows, k_new, c_pool, r_pool_t, page_table_t):
    b, rows, _ = q_rows.shape
    t_new = k_new.shape[1]
    n_pages = page_table_t.shape[0]
    n_split = 4 if n_pages % 4 == 0 else 1
    return pl.pallas_call(
        functools.partial(_mla_dec_kernel, n_pages=n_pages, t_new=t_new, n_split=n_split),
        grid_spec=pltpu.PrefetchScalarGridSpec(
            num_scalar_prefetch=1,
            grid=(b,),
            in_specs=[pl.BlockSpec((1, rows, C_CAT), lambda i, *_: (i, 0, 0)),
                      pl.BlockSpec((1, t_new, C_CAT), lambda i, *_: (i, 0, 0)),
                      pl.BlockSpec(memory_space=pl.ANY),
                      pl.BlockSpec(memory_space=pl.ANY)],
            out_specs=pl.BlockSpec((1, rows, C_KV_LORA), lambda i, *_: (i, 0, 0)),
            scratch_shapes=[pltpu.VMEM((2, n_pages, PAGE, C_KV_LORA), F32),
                            pltpu.VMEM((2, C_ROPE, n_pages * PAGE), F32),
                            pltpu.SemaphoreType.DMA((2, 2))]),
        out_shape=jax.ShapeDtypeStruct((b, rows, C_KV_LORA), BF16),
        compiler_params=_params("arbitrary"),
        name="mla_decode",
    )(page_table_t, q_rows, k_new, c_pool, r_pool_t)


def _mla_out_kernel(o_ref, wuv_ref, wo_ref, x_ref, y_ref):
    y_ref[...] = _mla_out([o_ref[hd] for hd in range(C_HEADS)], wuv_ref, wo_ref, x_ref[...])


def _mla_out_proj(o_heads, wuv, wo, x):
    m, d = x.shape
    return pl.pallas_call(
        _mla_out_kernel,
        grid=(1,),
        in_specs=[_resident(o_heads.shape), _resident(wuv.shape), _resident(wo.shape), _resident(x.shape)],
        out_specs=pl.BlockSpec((m, d), lambda i: (0, 0)),
        out_shape=jax.ShapeDtypeStruct((m, d), F32),
        compiler_params=_params("arbitrary"),
        name="mla_out_proj",
    )(o_heads, wuv, wo, x)


def _ffn_kernel(*refs, tm, tiles_per_seq, seq_in_tile, final_norm):
    x_ref, g_ref, wup_ref, wconv_ref, bconv_ref, wdown_ref = refs[:6]
    refs = refs[6:]
    if seq_in_tile:
        e1_ref, e2_ref = refs[:2]
        refs = refs[2:]
    if final_norm:
        gfin_ref = refs[0]
        refs = refs[1:]
    y_ref, gate_out_ref, gbuf = refs
    halo = 8

    if seq_in_tile:
        gbuf[0:halo, :] = jnp.zeros((halo, D_FF), F32)
    else:
        @pl.when(pl.program_id(0) % tiles_per_seq == 0)
        def _():
            gbuf[0:halo, :] = jnp.zeros((halo, D_FF), F32)

    x = x_ref[...]
    h = _rms(x, g_ref[...]).astype(BF16)
    gate = _dot(h, wup_ref[:, :D_FF])
    up = _dot(h, wup_ref[:, D_FF:])
    gbuf[halo:halo + tm, :] = gate
    tap0 = gbuf[halo - 2:halo - 2 + tm, :]
    tap1 = gbuf[halo - 1:halo - 1 + tm, :]
    if seq_in_tile:
        tok = lax.broadcasted_iota(jnp.int32, (tm, 1), 0) % seq_in_tile
        tap0 = jnp.where(tok < 2, e2_ref[...], tap0)
        tap1 = jnp.where(tok < 1, e1_ref[...], tap1)
        gate_out_ref[...] = gate
    else:
        gate_out_ref[0] = gate[tm - halo:, :]
        gbuf[0:halo, :] = gate[tm - halo:, :]
    conv = tap0 * wconv_ref[0:1, :] + bconv_ref[...] + tap1 * wconv_ref[1:2, :] + gate * wconv_ref[2:3, :]
    act = (jax.nn.silu(conv) * up).astype(BF16)
    y = x + _dot(act, wdown_ref[...])
    if final_norm:
        y = _rms(y, gfin_ref[...])
    y_ref[...] = y


def _conv_ffn(x, g, wup, wconv, bconv, wdown, *, tm, seq, conv_state=None, g_final=None):
    m, d = x.shape
    seq_in_tile = 0 if conv_state is None else seq
    row = pl.BlockSpec((tm, d), lambda i: (i, 0))
    wide = pl.BlockSpec((tm, D_FF), lambda i: (i, 0))
    args = [x, g.reshape(1, d), wup, wconv, bconv.reshape(1, D_FF), wdown]
    in_specs = [row, _resident((1, d)), _resident(wup.shape), _resident(wconv.shape),
                _resident((1, D_FF)), _resident(wdown.shape)]
    if conv_state is not None:
        assert tm % seq == 0 and seq >= 2
        zeros = jnp.zeros((m // seq, seq - 2, D_FF), F32)
        e2 = jnp.concatenate([conv_state, zeros], axis=1).reshape(m, D_FF)
        e1 = jnp.concatenate([conv_state[:, 1:], zeros, zeros[:, :1]], axis=1).reshape(m, D_FF)
        args += [e1, e2]
        in_specs += [wide, wide]
        gate_spec = wide
        gate_shape = jax.ShapeDtypeStruct((m, D_FF), F32)
    else:
        assert seq % tm == 0
        gate_spec = pl.BlockSpec((1, 8, D_FF), lambda i: (i // (seq // tm), 0, 0))
        gate_shape = jax.ShapeDtypeStruct((m // seq, 8, D_FF), F32)
    if g_final is not None:
        args.append(g_final.reshape(1, d))
        in_specs.append(_resident((1, d)))
    return pl.pallas_call(
        functools.partial(_ffn_kernel, tm=tm, tiles_per_seq=max(seq // tm, 1), seq_in_tile=seq_in_tile,
                          final_norm=g_final is not None),
        grid=(m // tm,),
        in_specs=in_specs,
        out_specs=[row, gate_spec],
        out_shape=[jax.ShapeDtypeStruct((m, d), F32), gate_shape],
        scratch_shapes=[pltpu.VMEM((tm + 8, D_FF), F32)],
        compiler_params=_params("arbitrary"),
        name="conv_ffn",
    )(*args)


def _prepare_weights(w_in_ab, w_out_ab, w_dq, w_uq, w_dkv, w_uk, w_uv, w_o_c, w_up, w_down):
    per_head = C_NOPE + C_ROPE
    uq = w_uq[0].reshape(C_Q_LORA, C_HEADS, per_head)
    wuqr = jnp.pad(uq[:, :, C_NOPE:], ((0, 0), (0, 0), (0, 128 - C_ROPE)))
    return dict(
        w_in=w_in_ab[0].astype(BF16),
        w_out=w_out_ab[0].astype(BF16),
        wdq=w_dq[0].astype(BF16),
        wuqn=uq[:, :, :C_NOPE].reshape(C_Q_LORA, C_HEADS * C_NOPE).astype(BF16),
        wuqr=wuqr.reshape(C_Q_LORA, C_HEADS * 128).astype(BF16),
        wdkv=jnp.pad(w_dkv[0], ((0, 0), (0, 128 - C_ROPE))).astype(BF16),
        wukt=w_uk[0].transpose(1, 2, 0).astype(BF16),
        wuv=w_uv[0].transpose(1, 0, 2).astype(BF16),
        wo=w_o_c[0].astype(BF16),
        w_up=[w_up[layer].astype(BF16) for layer in range(w_up.shape[0])],
        w_down=[w_down[layer].astype(BF16) for layer in range(w_down.shape[0])],
    )


def _alibi_slopes():
    return 2.0 ** (-8.0 * (jnp.arange(A_HEADS, dtype=F32) + 1.0) / A_HEADS)


AB_WIDTHS = (A_WIDTH,) * 3 + (B_WIDTH,) * 4


def kernel(x_prompt, x_sample, cache_a_k, cache_a_v, state_ret, cache_c_kv, cache_k_rope, state_ffn_conv,
           page_table, g_mix, g_ffn, g_final, w_in_ab, w_out_ab, g_ret, w_dq, g_q, w_uq, w_dkv, g_kv,
           w_uk, w_uv, w_o_c, w_up, w_conv, b_conv, w_down):
    assert g_mix.shape[0] == 2 and w_in_ab.shape[0] == 1 and w_dq.shape[0] == 1
    w = _prepare_weights(w_in_ab, w_out_ab, w_dq, w_uq, w_dkv, w_uk, w_uv, w_o_c, w_up, w_down)
    slopes = _alibi_slopes()
    tm_ffn = 256

    def ffn(x, layer, **kw):
        return _conv_ffn(x, g_ffn[layer], w["w_up"][layer], w_conv[layer], b_conv[layer], w["w_down"][layer], **kw)

    def mla_prep(x3, pos, tm):
        return _mla_prep(x3, pos, g_mix[1], w["wdq"], g_q[0], w["wuqn"], w["wuqr"], w["wdkv"], g_kv[0],
                         w["wukt"], tm=tm)

    b, s, d = x_prompt.shape
    m = b * s
    pos_p = jnp.arange(s)
    qa, ka, va, qb, kb, vb, gb, ka_t, va_t = _norm_proj(x_prompt, g_mix[0], w["w_in"], AB_WIDTHS, tm=512,
                                                        transposed=(1, 2))
    o_a = _attn_a_prompt(qa, ka, va, slopes)
    o_b, ret_p = _retention(qb, kb, vb, gb, pos_p, g_ret[0], jnp.zeros((b, B_HEADS, B_DIM, B_DIM), F32), rows=512)
    xp = _out_proj(o_a.reshape(m, -1), o_b.reshape(m, -1), w["w_out"], x_prompt.reshape(m, d), tm=1024)
    xp, tails0 = ffn(xp, 0, tm=tm_ffn, seq=s)
    qcat_p, kcat_p, c_kv_p, k_rope_t_p = mla_prep(xp.reshape(b, s, d), pos_p, 512)

    bs, t, _ = x_sample.shape
    ms = bs * t
    pos_s = PAST_LEN + jnp.arange(t)
    xs = x_sample.reshape(ms, d)
    sq, sk, sv, sqb, skb, svb, sgb, sk_t, sv_t = _norm_proj(xs.reshape(1, ms, d), g_mix[0], w["w_in"], AB_WIDTHS,
                                                            tm=ms, transposed=(1, 2))
    per_seq = lambda a: a.reshape(bs, t, -1)
    nbuf = cache_a_k.shape[2]
    feature_major = lambda a: a[0].transpose(0, 2, 3, 1).reshape(bs, A_WIDTH, nbuf)

    xp, so_a, new_k_s, new_v_s = _mla_flash_with_decode(
        qcat_p, kcat_p, xp.reshape(b, s, d), w["wuv"], w["wo"], slopes, per_seq(sq), per_seq(sk), per_seq(sv),
        sk_t[0], sv_t[0], feature_major(cache_a_k), feature_major(cache_a_v), tq=128, tk=512)
    y_prompt, tails1 = ffn(xp.reshape(m, d), 1, tm=tm_ffn, seq=s, g_final=g_final)

    so_b, ret_s = _retention_decode(per_seq(sqb), per_seq(skb), per_seq(svb), per_seq(sgb), pos_s, g_ret[0],
                                    state_ret[0], n_seq=4)
    xs = _out_proj(so_a.reshape(ms, -1), so_b.reshape(ms, -1), w["w_out"], xs, tm=ms)
    xs, gate0 = ffn(xs, 0, tm=ms, seq=t, conv_state=state_ffn_conv[0])
    qcat_s, kcat_s, c_kv_s, k_rope_t_s = mla_prep(xs.reshape(1, ms, d), jnp.tile(pos_s, bs), ms)
    q_rows = qcat_s.reshape(C_HEADS, bs, t, C_CAT).transpose(1, 0, 2, 3).reshape(bs, C_HEADS * t, C_CAT)
    o_lat = _mla_decode(q_rows, kcat_s.reshape(bs, t, C_CAT), cache_c_kv[0], cache_k_rope[0].transpose(0, 2, 1),
                        page_table.T)
    o_heads = o_lat.reshape(bs, C_HEADS, t, C_KV_LORA).transpose(1, 0, 2, 3).reshape(C_HEADS, ms, C_KV_LORA)
    xs = _mla_out_proj(o_heads, w["wuv"], w["wo"], xs)
    y_sample, gate1 = ffn(xs, 1, tm=ms, seq=t, conv_state=state_ffn_conv[1], g_final=g_final)

    keep = min(A_PATTERNS[-1][0], s)
    heads_p = lambda a: a[:, :, s - keep:].reshape(b, A_HEADS, A_HEAD_DIM, keep).transpose(0, 3, 1, 2)[None]
    heads_s = lambda a: a.reshape(bs, A_HEADS, A_HEAD_DIM, nbuf).transpose(0, 3, 1, 2)[None]
    last2 = lambda g: g.reshape(bs, t, D_FF)[:, t - 2:]
    return (y_prompt.reshape(b, s, d), y_sample.reshape(bs, t, d),
            heads_p(ka_t), heads_p(va_t), ret_p[None], c_kv_p[None], k_rope_t_p.transpose(0, 2, 1)[None],
            jnp.stack([tails0[:, 6:8], tails1[:, 6:8]]),
            heads_s(new_k_s), heads_s(new_v_s), ret_s[None], c_kv_s.reshape(1, bs, t, C_KV_LORA),
            k_rope_t_s.reshape(C_ROPE, bs, t).transpose(1, 2, 0)[None], jnp.stack([last2(gate0), last2(gate1)]))
```

```python
import functools

import jax
import jax.numpy as jnp
from jax import lax
from jax.experimental import pallas as pl
from jax.experimental.pallas import tpu as pltpu

F32 = jnp.float32
BF16 = jnp.bfloat16

D_MODEL = 1024
A_HEADS = 8
A_HEAD_DIM = 64
A_WIDTH = A_HEADS * A_HEAD_DIM
A_PATTERNS = ((128, 1), (512, 4), (2048, 16))
BAND = 128
B_HEADS = 4
B_DIM = 128
B_WIDTH = B_HEADS * B_DIM
RET_CHUNK = 128
C_HEADS = 8
C_Q_LORA = 384
C_KV_LORA = 256
C_NOPE = 128
C_ROPE = 64
C_V_DIM = 128
C_CAT = C_KV_LORA + 128
MLA_SCALE = (C_NOPE + C_ROPE) ** -0.5
D_FF = 2816
ROPE_BASE = 10000.0
EPS = 1e-6
PAST_LEN = 8192
PAGE = 128

VMEM_LIMIT = 56 * 1024 * 1024

NT_DIMS = (((1,), (1,)), ((), ()))
TN_DIMS = (((0,), (0,)), ((), ()))


def _params(*sem):
    return pltpu.CompilerParams(dimension_semantics=sem, vmem_limit_bytes=VMEM_LIMIT)


def _resident(shape):
    nd = len(shape)
    return pl.BlockSpec(shape, lambda *_: (0,) * nd, pipeline_mode=pl.Buffered(1))


def _rms(x, g):
    return x * lax.rsqrt(jnp.mean(x * x, axis=-1, keepdims=True) + EPS) * g


def _dot(a, b):
    return jnp.dot(a, b, preferred_element_type=F32)


def _dot_nt(a, b):
    return lax.dot_general(a, b, NT_DIMS, preferred_element_type=F32)


def _norm_proj_kernel(x_ref, g_ref, w_ref, *out_refs, n_groups, transposed):
    h = _rms(x_ref[0], g_ref[...]).astype(BF16)
    t_refs = dict(zip(transposed, out_refs[n_groups:]))
    off = 0
    for grp, o_ref in enumerate(out_refs[:n_groups]):
        n = o_ref.shape[-1]
        r = _dot(h, w_ref[:, off:off + n])
        o_ref[0] = r
        if grp in t_refs:
            t_refs[grp][0] = r.T
        off += n


def _norm_proj(x, g, w, widths, tm, transposed=()):
    b, s, d = x.shape
    return pl.pallas_call(
        functools.partial(_norm_proj_kernel, n_groups=len(widths), transposed=tuple(transposed)),
        grid=(b, s // tm),
        in_specs=[pl.BlockSpec((1, tm, d), lambda i, j: (i, j, 0)),
                  _resident((1, d)),
                  _resident(w.shape)],
        out_specs=([pl.BlockSpec((1, tm, n), lambda i, j: (i, j, 0)) for n in widths]
                   + [pl.BlockSpec((1, widths[t], tm), lambda i, j: (i, 0, j)) for t in transposed]),
        out_shape=([jax.ShapeDtypeStruct((b, s, n), F32) for n in widths]
                   + [jax.ShapeDtypeStruct((b, widths[t], s), F32) for t in transposed]),
        compiler_params=_params("parallel", "parallel"),
        name="norm_proj",
    )(x, g.reshape(1, d), w)


def _unrolled_loop(n_items, fn):
    unroll = next(u for u in (16, 15, 12, 8, 6, 5, 4, 3, 2, 1) if n_items % u == 0)
    if n_items <= unroll:
        for t in range(n_items):
            fn(t)
        return

    def body(it, carry):
        for u in range(unroll):
            fn(it * unroll + u)
        return carry

    lax.fori_loop(0, n_items // unroll, body, 0)


def _attn_a_kernel(slopes_ref, q_ref, k_ref, v_ref, o_ref, os_ref, ls_ref, bias_ref, *, seq):
    pair = pl.program_id(1)
    is_a = lax.broadcasted_iota(jnp.int32, (1, 128), 1) < A_HEAD_DIM

    row = lax.broadcasted_iota(jnp.int32, (2 * BAND, 2 * BAND), 0)
    col = lax.broadcasted_iota(jnp.int32, (2 * BAND, 2 * BAND), 1)
    rel = BAND + (row & (BAND - 1)) - col
    slope = jnp.where(row < BAND, slopes_ref[2 * pair], slopes_ref[2 * pair + 1])
    in_band = (rel >= 0) & (rel <= BAND)
    for pat, (_, dil) in enumerate(A_PATTERNS):
        bias_ref[pat] = jnp.where(in_band, -slope * (rel * dil).astype(F32), -jnp.inf)

    def block(pat, dil, q_start, k_start, first):
        nk = BAND if first else 2 * BAND
        q = q_ref[0, pl.ds(q_start, BAND, stride=dil), :] * (A_HEAD_DIM ** -0.5)
        k = k_ref[0, pl.ds(k_start, nk, stride=dil), :].astype(BF16)
        v = v_ref[0, pl.ds(k_start, nk, stride=dil), :].astype(BF16)
        q2 = jnp.concatenate([jnp.where(is_a, q, 0.0), jnp.where(is_a, 0.0, q)], axis=0).astype(BF16)
        s = _dot_nt(q2, k) + (bias_ref[pat, :, BAND:] if first else bias_ref[pat])
        m = jnp.max(s, axis=-1, keepdims=True)
        e = jnp.exp(s - m)
        l = jnp.sum(e, axis=-1, keepdims=True)
        o = _dot(e.astype(BF16), v) / l
        lse = m + jnp.log(l)
        os_ref[pat, pl.ds(q_start, BAND, stride=dil), :] = jnp.where(is_a, o[:BAND], o[BAND:])
        ls_ref[pat, pl.ds(q_start, BAND, stride=dil), :] = jnp.where(is_a, lse[:BAND], lse[BAND:])

    for pat, (_, dil) in enumerate(A_PATTERNS):
        nb = seq // dil // BAND

        def first_block(r, pat=pat, dil=dil):
            block(pat, dil, r, r, True)

        def later_block(t, pat=pat, dil=dil, nb=nb):
            r = t // (nb - 1)
            j = 1 + t % (nb - 1)
            block(pat, dil, r + dil * BAND * j, r + dil * BAND * (j - 1), False)

        _unrolled_loop(dil, first_block)
        if nb > 1:
            _unrolled_loop(dil * (nb - 1), later_block)

    rows = 256

    def combine(i, carry):
        sl = pl.ds(pl.multiple_of(i * rows, rows), rows)
        l0, l1, l2 = ls_ref[0, sl, :], ls_ref[1, sl, :], ls_ref[2, sl, :]
        mx = jnp.maximum(jnp.maximum(l0, l1), l2)
        w0, w1, w2 = jnp.exp(l0 - mx), jnp.exp(l1 - mx), jnp.exp(l2 - mx)
        num = w0 * os_ref[0, sl, :] + w1 * os_ref[1, sl, :] + w2 * os_ref[2, sl, :]
        o_ref[0, sl, :] = (num / (w0 + w1 + w2)).astype(o_ref.dtype)
        return carry

    lax.fori_loop(0, seq // rows, combine, 0)


def _attn_a_prompt(q, k, v, slopes):
    b, s, _ = q.shape
    assert s % (A_PATTERNS[-1][1] * BAND) == 0
    spec = pl.BlockSpec((1, s, 128), lambda i, p, *_: (i, 0, p))
    return pl.pallas_call(
        functools.partial(_attn_a_kernel, seq=s),
        grid_spec=pltpu.PrefetchScalarGridSpec(
            num_scalar_prefetch=1,
            grid=(b, A_WIDTH // 128),
            in_specs=[spec, spec, spec],
            out_specs=spec,
            scratch_shapes=[pltpu.VMEM((3, s, 128), F32), pltpu.VMEM((3, s, 128), F32),
                            pltpu.VMEM((3, 2 * BAND, 2 * BAND), F32)]),
        out_shape=jax.ShapeDtypeStruct((b, s, A_WIDTH), BF16),
        compiler_params=_params("parallel", "parallel"),
        name="attn_a_prompt",
    )(slopes, q, k, v)


def _pattern_multiplicity(delta):
    mult = jnp.zeros(delta.shape, F32)
    for window, dil in A_PATTERNS:
        hit = (delta >= 0) & (delta <= window)
        if dil > 1:
            hit = hit & ((delta & (dil - 1)) == 0)
        mult = mult + jnp.where(hit, 1.0, 0.0)
    return mult


def _attn_a_dec_body(slopes_ref, half, seq_in_group, q_ref, kn_ref, vn_ref, knt_ref, vnt_ref, ck_ref, cv_ref,
                     o_ref, nk_ref, nv_ref, *, nbuf, t_new):
    width = q_ref.shape[-1]
    heads = width // A_HEAD_DIM
    rows = heads * t_new
    q = q_ref[0]
    q_rows = jnp.concatenate([q] * heads, axis=0)
    row = lax.broadcasted_iota(jnp.int32, (rows, width), 0)
    lane = lax.broadcasted_iota(jnp.int32, (rows, width), 1)
    qm = jnp.where(row // t_new == lane // A_HEAD_DIM, q_rows, 0.0).astype(BF16)

    rcol = lax.broadcasted_iota(jnp.int32, (rows, 1), 0)
    slope = jnp.zeros((rows, 1), F32)
    for h in range(heads):
        slope = jnp.where(rcol // t_new == h, slopes_ref[half * heads + h], slope)
    tok = rcol % t_new

    s_c = _dot(qm, ck_ref[0].astype(BF16)) * (A_HEAD_DIM ** -0.5)
    pos = lax.broadcasted_iota(jnp.int32, (rows, nbuf), 1)
    delta_c = nbuf + tok - pos
    mult_c = _pattern_multiplicity(delta_c)
    s_c = jnp.where(mult_c > 0, s_c - slope * delta_c.astype(F32), -jnp.inf)
    qf = qm.astype(F32)
    knf = kn_ref[0].astype(BF16).astype(F32)
    vnf = vn_ref[0].astype(BF16).astype(F32)
    s_n, mult_n = [], []
    for t in range(t_new):
        delta = tok - t
        mult = _pattern_multiplicity(delta)
        sc = jnp.sum(qf * knf[t:t + 1, :], axis=-1, keepdims=True) * (A_HEAD_DIM ** -0.5)
        s_n.append(jnp.where(mult > 0, sc - slope * delta.astype(F32), -jnp.inf))
        mult_n.append(mult)
    m = jnp.max(s_c, axis=-1, keepdims=True)
    for sc in s_n:
        m = jnp.maximum(m, sc)
    e_c = jnp.exp(s_c - m) * mult_c
    l = jnp.sum(e_c, axis=-1, keepdims=True)
    acc = _dot_nt(e_c.astype(BF16), cv_ref[0].astype(BF16))
    for t in range(t_new):
        e = jnp.exp(s_n[t] - m) * mult_n[t]
        l = l + e
        acc = acc + e * vnf[t:t + 1, :]
    res = acc / l
    lane_head = lax.broadcasted_iota(jnp.int32, (t_new, width), 1) // A_HEAD_DIM
    out = jnp.zeros((t_new, width), F32)
    for h in range(heads):
        out = jnp.where(lane_head == h, res[h * t_new:(h + 1) * t_new], out)
    o_ref[0] = out.astype(o_ref.dtype)

    is_new = lax.broadcasted_iota(jnp.int32, (width, 128), 1) >= 128 - t_new
    to_tail = (128 - t_new - seq_in_group * t_new) % 128
    for c_ref, nt_ref, n_ref in ((ck_ref, knt_ref, nk_ref), (cv_ref, vnt_ref, nv_ref)):
        rolled = pltpu.roll(c_ref[0], nbuf - t_new, 1)
        n_ref[0, :, 0:nbuf - 128] = rolled[:, 0:nbuf - 128]
        n_ref[0, :, nbuf - 128:nbuf] = jnp.where(is_new, pltpu.roll(nt_ref[...], to_tail, 1),
                                                 rolled[:, nbuf - 128:nbuf])


def _retention_head(q, k, v, gate, rope_q, rope_k, dmask, qdec, kdec, cdec, g, state):
    (cos_q, sin_q), (cos_k, sin_k) = rope_q, rope_k
    qr = (q * cos_q + pltpu.roll(q, B_DIM // 2, 1) * sin_q).astype(BF16)
    kr = (k * cos_k + pltpu.roll(k, B_DIM // 2, 1) * sin_k) * (B_DIM ** -0.5)
    v = v.astype(BF16)
    att = _dot_nt(qr, kr.astype(BF16)) * dmask
    o = _dot(att.astype(BF16), v) + _dot(qr, state.astype(BF16)) * qdec
    kd = (kr * kdec).astype(BF16)
    new_state = state * cdec + lax.dot_general(kd, v, TN_DIMS, preferred_element_type=F32)
    d = o - jnp.mean(o, axis=-1, keepdims=True)
    y = d * lax.rsqrt(jnp.mean(d * d, axis=-1, keepdims=True) + EPS) * g
    return y * jax.nn.silu(gate), new_state


def _retention_kernel(q_ref, k_ref, v_ref, gate_ref, cos_ref, sin_ref, dmask_ref, qdec_ref, kdec_ref,
                      cdec_ref, g_ref, s0_ref, o_ref, s_ref, *, chunk, n_chunks):
    @pl.when(pl.program_id(1) == 0)
    def _():
        s_ref[...] = s0_ref[...]

    for c in range(n_chunks):
        rows = slice(c * chunk, (c + 1) * chunk)
        rope = (cos_ref[rows, :], sin_ref[rows, :])
        for h in range(B_HEADS):
            cols = slice(h * B_DIM, (h + 1) * B_DIM)
            y, s_ref[0, h] = _retention_head(
                q_ref[0, rows, cols], k_ref[0, rows, cols], v_ref[0, rows, cols], gate_ref[0, rows, cols],
                rope, rope, dmask_ref[h], qdec_ref[h], kdec_ref[h], cdec_ref[h], g_ref[:, cols], s_ref[0, h])
            o_ref[0, rows, cols] = y.astype(o_ref.dtype)


def _retention_dec_kernel(q_ref, k_ref, v_ref, gate_ref, cosq_ref, sinq_ref, cosk_ref, sink_ref, dmask_ref,
                          qdec_ref, kdec_ref, cdec_ref, g_ref, s0_ref, o_ref, s_ref, kpad, vpad, *, n_seq):
    t_rows = k_ref.shape[1]
    kpad[...] = jnp.zeros(kpad.shape, F32)
    vpad[...] = jnp.zeros(vpad.shape, F32)
    rope_q = (cosq_ref[...], sinq_ref[...])
    rope_k = (cosk_ref[...], sink_ref[...])
    for b in range(n_seq):
        kpad[b, 0:t_rows, :] = k_ref[b]
        vpad[b, 0:t_rows, :] = v_ref[b]
        for h in range(B_HEADS):
            cols = slice(h * B_DIM, (h + 1) * B_DIM)
            y, s_ref[b, h] = _retention_head(
                q_ref[b, :, cols], kpad[b, :, cols], vpad[b, :, cols], gate_ref[b, :, cols],
                rope_q, rope_k, dmask_ref[h], qdec_ref[h], kdec_ref[h], cdec_ref[h], g_ref[:, cols], s0_ref[b, h])
            o_ref[b, :, cols] = y.astype(o_ref.dtype)


def _retention_tables(chunk, q_pad, k_pad):
    lg = jnp.log(1.0 - 2.0 ** (-5.0 - jnp.arange(B_HEADS, dtype=F32)))
    qi = jnp.arange(q_pad, dtype=F32)
    ki = jnp.arange(k_pad, dtype=F32)
    rel = qi[:, None] - ki[None, :]
    live = (rel >= 0) & (qi < chunk)[:, None] & (ki < chunk)[None, :]
    dmask = jnp.where(live, jnp.exp(jnp.maximum(rel, 0.0)[None] * lg[:, None, None]), 0.0)
    qdec = jnp.where((qi < chunk)[None, :], jnp.exp((qi + 1.0)[None, :] * lg[:, None]), 0.0)
    kdec = jnp.where((ki < chunk)[None, :], jnp.exp((chunk - 1.0 - ki)[None, :] * lg[:, None]), 0.0)
    cdec = jnp.exp(chunk * lg)
    bc = lambda a: jnp.broadcast_to(a[..., None], a.shape + (B_DIM,))
    return dmask, bc(qdec), bc(kdec), bc(cdec[:, None])


def _rope_tables(pos, dim, width):
    half = dim // 2
    inv = ROPE_BASE ** (-jnp.arange(half, dtype=F32) / half)
    ang = pos.astype(F32)[:, None] * inv[None, :]
    cos, sin = jnp.cos(ang), jnp.sin(ang)
    zeros = jnp.zeros((pos.shape[0], width - dim), F32)
    return (jnp.concatenate([cos, cos, zeros], axis=1), jnp.concatenate([-sin, sin, zeros], axis=1))


def _retention_decode(q, k, v, gate, pos, g_ret, s0, n_seq):
    b, t, _ = q.shape
    t_pad = 8
    assert t <= t_pad and b % n_seq == 0
    pad_rows = lambda a: jnp.pad(a, ((0, 0), (0, t_pad - t), (0, 0)))
    cosk, sink = _rope_tables(jnp.pad(pos, (0, RET_CHUNK - t)), B_DIM, B_DIM)
    dmask, qdec, kdec, cdec = _retention_tables(t, t_pad, RET_CHUNK)
    tok = pl.BlockSpec((n_seq, t_pad, B_WIDTH), lambda i: (i, 0, 0))
    st = pl.BlockSpec((n_seq, B_HEADS, B_DIM, B_DIM), lambda i: (i, 0, 0, 0))
    consts = (cosk[:t_pad], sink[:t_pad], cosk, sink, dmask, qdec, kdec, cdec, g_ret.reshape(1, B_WIDTH))
    o, s_new = pl.pallas_call(
        functools.partial(_retention_dec_kernel, n_seq=n_seq),
        grid=(b // n_seq,),
        in_specs=[tok, tok, tok, tok] + [_resident(c.shape) for c in consts] + [st],
        out_specs=[tok, st],
        out_shape=[jax.ShapeDtypeStruct((b, t_pad, B_WIDTH), BF16),
                   jax.ShapeDtypeStruct((b, B_HEADS, B_DIM, B_DIM), F32)],
        scratch_shapes=[pltpu.VMEM((n_seq, RET_CHUNK, B_WIDTH), F32),
                        pltpu.VMEM((n_seq, RET_CHUNK, B_WIDTH), F32)],
        compiler_params=_params("parallel"),
        name="retention_decode",
    )(pad_rows(q), pad_rows(k), pad_rows(v), pad_rows(gate), *consts, s0)
    return o[:, :t], s_new


def _retention(q, k, v, gate, pos, g_ret, s0, rows):
    b, s, _ = q.shape
    cos, sin = _rope_tables(pos, B_DIM, B_DIM)
    dmask, qdec, kdec, cdec = _retention_tables(RET_CHUNK, RET_CHUNK, RET_CHUNK)
    tok = pl.BlockSpec((1, rows, B_WIDTH), lambda i, j: (i, j, 0))
    tab = pl.BlockSpec((rows, B_DIM), lambda i, j: (j, 0))
    st = pl.BlockSpec((1, B_HEADS, B_DIM, B_DIM), lambda i, j: (i, 0, 0, 0))
    return pl.pallas_call(
        functools.partial(_retention_kernel, chunk=RET_CHUNK, n_chunks=rows // RET_CHUNK),
        grid=(b, s // rows),
        in_specs=[tok, tok, tok, tok, tab, tab, _resident(dmask.shape), _resident(qdec.shape),
                  _resident(kdec.shape), _resident(cdec.shape), _resident((1, B_WIDTH)), st],
        out_specs=[tok, st],
        out_shape=[jax.ShapeDtypeStruct((b, s, B_WIDTH), BF16),
                   jax.ShapeDtypeStruct((b, B_HEADS, B_DIM, B_DIM), F32)],
        compiler_params=_params("parallel", "arbitrary"),
        name="retention",
    )(q, k, v, gate, cos, sin, dmask, qdec, kdec, cdec, g_ret.reshape(1, B_WIDTH), s0)


def _out_proj_kernel(a_ref, b_ref, w_ref, x_ref, y_ref):
    na = a_ref.shape[-1]
    y_ref[...] = x_ref[...] + _dot(a_ref[...], w_ref[:na, :]) + _dot(b_ref[...], w_ref[na:, :])


def _out_proj(a, b, w, x, tm):
    m, d = x.shape
    return pl.pallas_call(
        _out_proj_kernel,
        grid=(m // tm,),
        in_specs=[pl.BlockSpec((tm, a.shape[1]), lambda i: (i, 0)),
                  pl.BlockSpec((tm, b.shape[1]), lambda i: (i, 0)),
                  _resident(w.shape),
                  pl.BlockSpec((tm, d), lambda i: (i, 0))],
        out_specs=pl.BlockSpec((tm, d), lambda i: (i, 0)),
        out_shape=jax.ShapeDtypeStruct((m, d), F32),
        compiler_params=_params("parallel"),
        name="out_proj",
    )(a, b, w, x)


def _rope_group(x, cos, sin):
    return x * cos + (pltpu.roll(x, C_ROPE // 2, 1) + pltpu.roll(x, 128 - C_ROPE // 2, 1)) * sin


def _mla_prep_kernel(x_ref, g_ref, wdq_ref, gq_ref, wuqn_ref, wuqr_ref, wdkv_ref, gkv_ref, wukt_ref,
                     cos_ref, sin_ref, qcat_ref, kcat_ref, ckv_ref, kr_ref):
    h = _rms(x_ref[0], g_ref[...]).astype(BF16)
    cq = _rms(_dot(h, wdq_ref[...]), gq_ref[...]).astype(BF16)
    q_nope = _dot(cq, wuqn_ref[...])
    q_rope = _dot(cq, wuqr_ref[...])
    kv = _dot(h, wdkv_ref[...])
    cos = cos_ref[...]
    sin = sin_ref[...]
    c_kv = _rms(kv[:, :C_KV_LORA], gkv_ref[...])
    k_rope = _rope_group(kv[:, C_KV_LORA:], cos, sin)
    ckv_ref[0] = c_kv
    kr_ref[0] = k_rope.T[:C_ROPE, :]
    kcat_ref[0, :, :C_KV_LORA] = c_kv.astype(BF16)
    kcat_ref[0, :, C_KV_LORA:] = k_rope.astype(BF16)
    for hd in range(C_HEADS):
        cols = slice(hd * 128, (hd + 1) * 128)
        qcat_ref[0, hd, :, :C_KV_LORA] = _dot(q_nope[:, cols].astype(BF16), wukt_ref[hd]).astype(BF16)
        qcat_ref[0, hd, :, C_KV_LORA:] = _rope_group(q_rope[:, cols], cos, sin).astype(BF16)


def _mla_prep(x, pos, g, wdq, gq, wuqn, wuqr, wdkv, gkv, wukt, tm):
    b, s, d = x.shape
    cos, sin = _rope_tables(pos, C_ROPE, 128)
    tab = pl.BlockSpec((tm, 128), lambda i, j: (j, 0))
    return pl.pallas_call(
        _mla_prep_kernel,
        grid=(b, s // tm),
        in_specs=[pl.BlockSpec((1, tm, d), lambda i, j: (i, j, 0)),
                  _resident((1, d)), _resident(wdq.shape), _resident((1, C_Q_LORA)),
                  _resident(wuqn.shape), _resident(wuqr.shape), _resident(wdkv.shape),
                  _resident((1, C_KV_LORA)), _resident(wukt.shape), tab, tab],
        out_specs=[pl.BlockSpec((1, C_HEADS, tm, C_CAT), lambda i, j: (i, 0, j, 0)),
                   pl.BlockSpec((1, tm, C_CAT), lambda i, j: (i, j, 0)),
                   pl.BlockSpec((1, tm, C_KV_LORA), lambda i, j: (i, j, 0)),
                   pl.BlockSpec((1, C_ROPE, tm), lambda i, j: (i, 0, j))],
        out_shape=[jax.ShapeDtypeStruct((b, C_HEADS, s, C_CAT), BF16),
                   jax.ShapeDtypeStruct((b, s, C_CAT), BF16),
                   jax.ShapeDtypeStruct((b, s, C_KV_LORA), F32),
                   jax.ShapeDtypeStruct((b, C_ROPE, s), F32)],
        compiler_params=_params("parallel", "parallel"),
        name="mla_prep",
    )(x, g.reshape(1, d), wdq, gq.reshape(1, -1), wuqn, wuqr, wdkv, gkv.reshape(1, -1), wukt, cos, sin)


def _mla_out(o_lat_heads, wuv_ref, wo_ref, x):
    o = jnp.concatenate([_dot(o_lat_heads[hd].astype(BF16), wuv_ref[hd]) for hd in range(C_HEADS)], axis=1)
    return x + _dot(o.astype(BF16), wo_ref[...])


def _mla_flash_blocks(i, q_ref, k_ref, m_ref, l_ref, acc_ref, *, tq, tk):
    last = (i * tq + tq - 1) // tk
    c_exp = MLA_SCALE * 1.4426950408889634
    rows = 512
    group = rows // tq

    m_ref[...] = jnp.full(m_ref.shape, -jnp.inf, F32)
    l_ref[...] = jnp.zeros(l_ref.shape, F32)
    acc_ref[...] = jnp.zeros(acc_ref.shape, F32)

    def key_block(j, masked, width=tk):
        k = k_ref[0, pl.ds(pl.multiple_of(j * tk, tk), width), :]
        n_chunks = width // 128
        if masked:
            qpos = i * tq + lax.broadcasted_iota(jnp.int32, (rows, 128), 0) % tq
            lane = lax.broadcasted_iota(jnp.int32, (rows, 128), 1)
        for g in range(C_HEADS // group):
            hs = slice(g * group, (g + 1) * group)
            s = _dot_nt(q_ref[0, hs].reshape(rows, C_CAT), k)
            chunks = [s[:, c * 128:(c + 1) * 128] for c in range(n_chunks)]
            if masked:
                chunks = [jnp.where(j * tk + c * 128 + lane <= qpos, sc, -jnp.inf)
                          for c, sc in enumerate(chunks)]
            part = functools.reduce(jnp.maximum, chunks)
            m_old = m_ref[hs].reshape(rows, 128)
            m_new = jnp.maximum(m_old, jnp.max(part, axis=-1, keepdims=True))
            alpha = jnp.exp2((m_old - m_new) * c_exp)
            p = [jnp.exp2((sc - m_new) * c_exp) for sc in chunks]
            row_sum = jnp.sum(functools.reduce(jnp.add, p), axis=-1, keepdims=True)
            l_ref[hs] = (alpha * l_ref[hs].reshape(rows, 128) + row_sum).reshape(group, tq, 128)
            m_ref[hs] = m_new.reshape(group, tq, 128)
            pv = _dot(jnp.concatenate(p, axis=1).astype(BF16), k[:, :C_KV_LORA])
            acc = jnp.concatenate([alpha] * (C_KV_LORA // 128), axis=1) * acc_ref[hs].reshape(rows, C_KV_LORA)
            acc_ref[hs] = (acc + pv).reshape(group, tq, C_KV_LORA)

    def body(j, carry):
        key_block(j, masked=False)
        return carry

    lax.fori_loop(0, last, body, 0)
    ratio = tk // tq
    for r in range(ratio):
        @pl.when(i % ratio == r)
        def _(r=r):
            key_block(last, masked=True, width=(r + 1) * tq)


def _mla_flash_finish(x_ref, wuv_ref, wo_ref, y_ref, l_ref, acc_ref):
    heads = []
    for hd in range(C_HEADS):
        inv = 1.0 / l_ref[hd]
        heads.append(acc_ref[hd] * jnp.concatenate([inv] * (C_KV_LORA // 128), axis=1))
    y_ref[0] = _mla_out(heads, wuv_ref, wo_ref, x_ref[0])


def _flash_decode_kernel(slopes_ref, q_ref, k_ref, x_ref, wuv_ref, wo_ref,
                         dq_ref, dkn_ref, dvn_ref, dknt_ref, dvnt_ref, dck_ref, dcv_ref,
                         y_ref, do_ref, dnk_ref, dnv_ref, m_ref, l_ref, acc_ref,
                         *, tq, tk, nbuf, t_new):
    i = pl.program_id(1)
    step = pl.program_id(0) * pl.num_programs(1) + i
    _mla_flash_blocks(i, q_ref, k_ref, m_ref, l_ref, acc_ref, tq=tq, tk=tk)
    _attn_a_dec_body(slopes_ref, step % 2, (step // 2) % (128 // t_new), dq_ref, dkn_ref, dvn_ref, dknt_ref,
                     dvnt_ref, dck_ref, dcv_ref, do_ref, dnk_ref, dnv_ref, nbuf=nbuf, t_new=t_new)
    _mla_flash_finish(x_ref, wuv_ref, wo_ref, y_ref, l_ref, acc_ref)


def _mla_flash_with_decode(qcat, kcat, x, wuv, wo, slopes, dq, dkn, dvn, dkn_t, dvn_t, cache_kt, cache_vt,
                           tq, tk):
    b, s, d = x.shape
    bs, t_new, _ = dq.shape
    nbuf = cache_kt.shape[2]
    nq = s // tq
    width = A_WIDTH // 2
    assert b * nq == 2 * bs and 128 % t_new == 0
    per_group = 128 // t_new
    seq = lambda i, j: (i * nq + j) // 2
    half = lambda i, j: (i * nq + j) % 2
    small = pl.BlockSpec((1, t_new, width), lambda i, j, *_: (seq(i, j), 0, half(i, j)))
    newt = pl.BlockSpec((width, 128), lambda i, j, *_: (half(i, j), seq(i, j) // per_group))
    big = pl.BlockSpec((1, width, nbuf), lambda i, j, *_: (seq(i, j), half(i, j), 0))
    return pl.pallas_call(
        functools.partial(_flash_decode_kernel, tq=tq, tk=tk, nbuf=nbuf, t_new=t_new),
        grid_spec=pltpu.PrefetchScalarGridSpec(
            num_scalar_prefetch=1,
            grid=(b, nq),
            in_specs=[pl.BlockSpec((1, C_HEADS, tq, C_CAT), lambda i, j, *_: (i, 0, j, 0)),
                      pl.BlockSpec((1, s, C_CAT), lambda i, j, *_: (i, 0, 0)),
                      pl.BlockSpec((1, tq, d), lambda i, j, *_: (i, j, 0)),
                      _resident(wuv.shape), _resident(wo.shape),
                      small, small, small, newt, newt, big, big],
            out_specs=[pl.BlockSpec((1, tq, d), lambda i, j, *_: (i, j, 0)), small, big, big],
            scratch_shapes=[pltpu.VMEM((C_HEADS, tq, 128), F32), pltpu.VMEM((C_HEADS, tq, 128), F32),
                            pltpu.VMEM((C_HEADS, tq, C_KV_LORA), F32)]),
        out_shape=[jax.ShapeDtypeStruct((b, s, d), F32),
                   jax.ShapeDtypeStruct((bs, t_new, A_WIDTH), BF16),
                   jax.ShapeDtypeStruct(cache_kt.shape, F32),
                   jax.ShapeDtypeStruct(cache_vt.shape, F32)],
        compiler_params=_params("parallel", "parallel"),
        name="mla_flash_with_decode",
    )(slopes, qcat, kcat, x, wuv, wo, dq, dkn, dvn, dkn_t, dvn_t, cache_kt, cache_vt)


def _mla_dec_kernel(pt_ref, q_ref, kn_ref, cpool_ref, rpool_ref, o_ref, cbuf, rbuf, sem,
                    *, n_pages, t_new, n_split):
    b = pl.program_id(0)
    slot = b % 2
    per_split = n_pages // n_split

    def page_copies(batch, sl, pg):
        pid = pt_ref[pg, batch]
        return (pltpu.make_async_copy(cpool_ref.at[pid], cbuf.at[sl, pg], sem.at[0, sl]),
                pltpu.make_async_copy(rpool_ref.at[pid], rbuf.at[sl, :, pg * PAGE:(pg + 1) * PAGE],
                                      sem.at[1, sl]))

    def start_batch(batch, sl):
        for pg in range(n_pages):
            for cp in page_copies(batch, sl, pg):
                cp.start()

    @pl.when(b == 0)
    def _():
        start_batch(0, 0)

    @pl.when(b + 1 < pl.num_programs(0))
    def _():
        start_batch(b + 1, 1 - slot)

    q = q_ref[0]
    rows = q.shape[0]
    q_lat = q[:, :C_KV_LORA]
    q_rope = q[:, C_KV_LORA:C_KV_LORA + C_ROPE]

    for pg in range(n_pages):
        for cp in page_copies(b, slot, pg):
            cp.wait()

    parts = []
    for part in range(n_split):
        keys = per_split * PAGE
        c_part = cbuf[slot, part * per_split:(part + 1) * per_split].reshape(keys, C_KV_LORA).astype(BF16)
        r_part = rbuf[slot, :, part * keys:(part + 1) * keys].astype(BF16)
        s = (_dot_nt(q_lat, c_part) + _dot(q_rope, r_part)) * MLA_SCALE
        m = jnp.max(s, axis=-1, keepdims=True)
        p = jnp.exp(s - m)
        parts.append((m, jnp.sum(p, axis=-1, keepdims=True), _dot(p.astype(BF16), c_part)))

    qf = q.astype(F32)
    knf = kn_ref[0].astype(F32)
    tok = lax.broadcasted_iota(jnp.int32, (rows, 1), 0) % t_new
    s_n = []
    for t in range(t_new):
        sc = jnp.sum(qf * knf[t:t + 1, :], axis=-1, keepdims=True) * MLA_SCALE
        s_n.append(jnp.where(tok >= t, sc, -jnp.inf))

    m = functools.reduce(jnp.maximum, [pm for pm, _, _ in parts] + s_n)
    l = jnp.zeros((rows, 1), F32)
    acc = jnp.zeros((rows, C_KV_LORA), F32)
    for part_m, part_l, part_acc in parts:
        w = jnp.exp(part_m - m)
        l = l + w * part_l
        acc = acc + w * part_acc
    for t in range(t_new):
        e = jnp.exp(s_n[t] - m)
        l = l + e
        acc = acc + e * knf[t:t + 1, :C_KV_LORA]
    o_ref[0] = (acc / l).astype(o_ref.dtype)


def _mla_decode(q_rows, k_new, c_pool, r_pool_t, page_table_t):
    b, rows, _ = q_rows.shape
    t_new = k_new.shape[1]
    n_pages = page_table_t.shape[0]
    n_split = 4 if n_pages % 4 == 0 else 1
    return pl.pallas_call(
        functools.partial(_mla_dec_kernel, n_pages=n_pages, t_new=t_new, n_split=n_split),
        grid_spec=pltpu.PrefetchScalarGridSpec(
            num_scalar_prefetch=1,
            grid=(b,),
            in_specs=[pl.BlockSpec((1, rows, C_CAT), lambda i, *_: (i, 0, 0)),
                      pl.BlockSpec((1, t_new, C_CAT), lambda i, *_: (i, 0, 0)),
                      pl.BlockSpec(memory_space=pl.ANY),
                      pl.BlockSpec(memory_space=pl.ANY)],
            out_specs=pl.BlockSpec((1, rows, C_KV_LORA), lambda i, *_: (i, 0, 0)),
            scratch_shapes=[pltpu.VMEM((2, n_pages, PAGE, C_KV_LORA), F32),
                            pltpu.VMEM((2, C_ROPE, n_pages * PAGE), F32),
                            pltpu.SemaphoreType.DMA((2, 2))]),
        out_shape=jax.ShapeDtypeStruct((b, rows, C_KV_LORA), BF16),
        compiler_params=_params("arbitrary"),
        name="mla_decode",
    )(page_table_t, q_rows, k_new, c_pool, r_pool_t)


def _mla_out_kernel(o_ref, wuv_ref, wo_ref, x_ref, y_ref):
    y_ref[...] = _mla_out([o_ref[hd] for hd in range(C_HEADS)], wuv_ref, wo_ref, x_ref[...])


def _mla_out_proj(o_heads, wuv, wo, x):
    m, d = x.shape
    return pl.pallas_call(
        _mla_out_kernel,
        grid=(1,),
        in_specs=[_resident(o_heads.shape), _resident(wuv.shape), _resident(wo.shape), _resident(x.shape)],
        out_specs=pl.BlockSpec((m, d), lambda i: (0, 0)),
        out_shape=jax.ShapeDtypeStruct((m, d), F32),
        compiler_params=_params("arbitrary"),
        name="mla_out_proj",
    )(o_heads, wuv, wo, x)


def _ffn_kernel(*refs, tm, tiles_per_seq, seq_in_tile, final_norm):
    x_ref, g_ref, wup_ref, wconv_ref, bconv_ref, wdown_ref = refs[:6]
    refs = refs[6:]
    if seq_in_tile:
        e1_ref, e2_ref = refs[:2]
        refs = refs[2:]
    if final_norm:
        gfin_ref = refs[0]
        refs = refs[1:]
    y_ref, gate_out_ref, gbuf = refs
    halo = 8

    if seq_in_tile:
        gbuf[0:halo, :] = jnp.zeros((halo, D_FF), F32)
    else:
        @pl.when(pl.program_id(0) % tiles_per_seq == 0)
        def _():
            gbuf[0:halo, :] = jnp.zeros((halo, D_FF), F32)

    x = x_ref[...]
    h = _rms(x, g_ref[...]).astype(BF16)
    gate = _dot(h, wup_ref[:, :D_FF])
    up = _dot(h, wup_ref[:, D_FF:])
    gbuf[halo:halo + tm, :] = gate
    tap0 = gbuf[halo - 2:halo - 2 + tm, :]
    tap1 = gbuf[halo - 1:halo - 1 + tm, :]
    if seq_in_tile:
        tok = lax.broadcasted_iota(jnp.int32, (tm, 1), 0) % seq_in_tile
        tap0 = jnp.where(tok < 2, e2_ref[...], tap0)
        tap1 = jnp.where(tok < 1, e1_ref[...], tap1)
        gate_out_ref[...] = gate
    else:
        gate_out_ref[0] = gate[tm - halo:, :]
        gbuf[0:halo, :] = gate[tm - halo:, :]
    conv = tap0 * wconv_ref[0:1, :] + bconv_ref[...] + tap1 * wconv_ref[1:2, :] + gate * wconv_ref[2:3, :]
    act = (jax.nn.silu(conv) * up).astype(BF16)
    y = x + _dot(act, wdown_ref[...])
    if final_norm:
        y = _rms(y, gfin_ref[...])
    y_ref[...] = y


def _conv_ffn(x, g, wup, wconv, bconv, wdown, layer, *, tm, seq, conv_state=None, g_final=None):
    m, d = x.shape
    seq_in_tile = 0 if conv_state is None else seq
    row = pl.BlockSpec((tm, d), lambda i: (i, 0))
    wide = pl.BlockSpec((tm, D_FF), lambda i: (i, 0))
    of_layer = lambda a: pl.BlockSpec((None,) + a.shape[1:], lambda i: (layer, 0, 0), pipeline_mode=pl.Buffered(1))
    args = [x, g.reshape(1, d), wup, wconv, bconv.reshape(1, D_FF), wdown]
    in_specs = [row, _resident((1, d)), of_layer(wup), _resident(wconv.shape),
                _resident((1, D_FF)), of_layer(wdown)]
    if conv_state is not None:
        assert tm % seq == 0 and seq >= 2
        zeros = jnp.zeros((m // seq, seq - 2, D_FF), F32)
        e2 = jnp.concatenate([conv_state, zeros], axis=1).reshape(m, D_FF)
        e1 = jnp.concatenate([conv_state[:, 1:], zeros, zeros[:, :1]], axis=1).reshape(m, D_FF)
        args += [e1, e2]
        in_specs += [wide, wide]
        gate_spec = wide
        gate_shape = jax.ShapeDtypeStruct((m, D_FF), F32)
    else:
        assert seq % tm == 0
        gate_spec = pl.BlockSpec((1, 8, D_FF), lambda i: (i // (seq // tm), 0, 0))
        gate_shape = jax.ShapeDtypeStruct((m // seq, 8, D_FF), F32)
    if g_final is not None:
        args.append(g_final.reshape(1, d))
        in_specs.append(_resident((1, d)))
    return pl.pallas_call(
        functools.partial(_ffn_kernel, tm=tm, tiles_per_seq=max(seq // tm, 1), seq_in_tile=seq_in_tile,
                          final_norm=g_final is not None),
        grid=(m // tm,),
        in_specs=in_specs,
        out_specs=[row, gate_spec],
        out_shape=[jax.ShapeDtypeStruct((m, d), F32), gate_shape],
        scratch_shapes=[pltpu.VMEM((tm + 8, D_FF), F32)],
        compiler_params=_params("arbitrary"),
        name="conv_ffn",
    )(*args)


def _prepare_weights(w_in_ab, w_out_ab, w_dq, w_uq, w_dkv, w_uk, w_uv, w_o_c, w_up, w_down):
    per_head = C_NOPE + C_ROPE
    uq = w_uq[0].reshape(C_Q_LORA, C_HEADS, per_head)
    wuqr = jnp.pad(uq[:, :, C_NOPE:], ((0, 0), (0, 0), (0, 128 - C_ROPE)))
    return dict(
        w_in=w_in_ab[0].astype(BF16),
        w_out=w_out_ab[0].astype(BF16),
        wdq=w_dq[0].astype(BF16),
        wuqn=uq[:, :, :C_NOPE].reshape(C_Q_LORA, C_HEADS * C_NOPE).astype(BF16),
        wuqr=wuqr.reshape(C_Q_LORA, C_HEADS * 128).astype(BF16),
        wdkv=jnp.pad(w_dkv[0], ((0, 0), (0, 128 - C_ROPE))).astype(BF16),
        wukt=w_uk[0].transpose(1, 2, 0).astype(BF16),
        wuv=w_uv[0].transpose(1, 0, 2).astype(BF16),
        wo=w_o_c[0].astype(BF16),
        w_up=w_up.astype(BF16),
        w_down=w_down.astype(BF16),
    )


def _alibi_slopes():
    return 2.0 ** (-8.0 * (jnp.arange(A_HEADS, dtype=F32) + 1.0) / A_HEADS)


AB_WIDTHS = (A_WIDTH,) * 3 + (B_WIDTH,) * 4


def kernel(x_prompt, x_sample, cache_a_k, cache_a_v, state_ret, cache_c_kv, cache_k_rope, state_ffn_conv,
           page_table, g_mix, g_ffn, g_final, w_in_ab, w_out_ab, g_ret, w_dq, g_q, w_uq, w_dkv, g_kv,
           w_uk, w_uv, w_o_c, w_up, w_conv, b_conv, w_down):
    assert g_mix.shape[0] == 2 and w_in_ab.shape[0] == 1 and w_dq.shape[0] == 1
    w = _prepare_weights(w_in_ab, w_out_ab, w_dq, w_uq, w_dkv, w_uk, w_uv, w_o_c, w_up, w_down)
    slopes = _alibi_slopes()
    tm_ffn = 512

    def ffn(x, layer, **kw):
        return _conv_ffn(x, g_ffn[layer], w["w_up"], w_conv[layer], b_conv[layer], w["w_down"], layer, **kw)

    def mla_prep(x3, pos, tm):
        return _mla_prep(x3, pos, g_mix[1], w["wdq"], g_q[0], w["wuqn"], w["wuqr"], w["wdkv"], g_kv[0],
                         w["wukt"], tm=tm)

    b, s, d = x_prompt.shape
    m = b * s
    pos_p = jnp.arange(s)
    qa, ka, va, qb, kb, vb, gb, ka_t, va_t = _norm_proj(x_prompt, g_mix[0], w["w_in"], AB_WIDTHS, tm=512,
                                                        transposed=(1, 2))
    o_a = _attn_a_prompt(qa, ka, va, slopes)
    o_b, ret_p = _retention(qb, kb, vb, gb, pos_p, g_ret[0], jnp.zeros((b, B_HEADS, B_DIM, B_DIM), F32), rows=512)
    xp = _out_proj(o_a.reshape(m, -1), o_b.reshape(m, -1), w["w_out"], x_prompt.reshape(m, d), tm=1024)
    xp, tails0 = ffn(xp, 0, tm=tm_ffn, seq=s)
    qcat_p, kcat_p, c_kv_p, k_rope_t_p = mla_prep(xp.reshape(b, s, d), pos_p, 512)

    bs, t, _ = x_sample.shape
    ms = bs * t
    pos_s = PAST_LEN + jnp.arange(t)
    xs = x_sample.reshape(ms, d)
    sq, sk, sv, sqb, skb, svb, sgb, sk_t, sv_t = _norm_proj(xs.reshape(1, ms, d), g_mix[0], w["w_in"], AB_WIDTHS,
                                                            tm=ms, transposed=(1, 2))
    per_seq = lambda a: a.reshape(bs, t, -1)
    nbuf = cache_a_k.shape[2]
    feature_major = lambda a: a[0].transpose(0, 2, 3, 1).reshape(bs, A_WIDTH, nbuf)

    xp, so_a, new_k_s, new_v_s = _mla_flash_with_decode(
        qcat_p, kcat_p, xp.reshape(b, s, d), w["wuv"], w["wo"], slopes, per_seq(sq), per_seq(sk), per_seq(sv),
        sk_t[0], sv_t[0], feature_major(cache_a_k), feature_major(cache_a_v), tq=128, tk=512)
    y_prompt, tails1 = ffn(xp.reshape(m, d), 1, tm=tm_ffn, seq=s, g_final=g_final)

    so_b, ret_s = _retention_decode(per_seq(sqb), per_seq(skb), per_seq(svb), per_seq(sgb), pos_s, g_ret[0],
                                    state_ret[0], n_seq=4)
    xs = _out_proj(so_a.reshape(ms, -1), so_b.reshape(ms, -1), w["w_out"], xs, tm=ms)
    xs, gate0 = ffn(xs, 0, tm=ms, seq=t, conv_state=state_ffn_conv[0])
    qcat_s, kcat_s, c_kv_s, k_rope_t_s = mla_prep(xs.reshape(1, ms, d), jnp.tile(pos_s, bs), ms)
    q_rows = qcat_s.reshape(C_HEADS, bs, t, C_CAT).transpose(1, 0, 2, 3).reshape(bs, C_HEADS * t, C_CAT)
    o_lat = _mla_decode(q_rows, kcat_s.reshape(bs, t, C_CAT), cache_c_kv[0], cache_k_rope[0].transpose(0, 2, 1),
                        page_table.T)
    o_heads = o_lat.reshape(bs, C_HEADS, t, C_KV_LORA).transpose(1, 0, 2, 3).reshape(C_HEADS, ms, C_KV_LORA)
    xs = _mla_out_proj(o_heads, w["wuv"], w["wo"], xs)
    y_sample, gate1 = ffn(xs, 1, tm=ms, seq=t, conv_state=state_ffn_conv[1], g_final=g_final)

    keep = min(A_PATTERNS[-1][0], s)
    heads_p = lambda a: a[:, :, s - keep:].reshape(b, A_HEADS, A_HEAD_DIM, keep).transpose(0, 3, 1, 2)[None]
    heads_s = lambda a: a.reshape(bs, A_HEADS, A_HEAD_DIM, nbuf).transpose(0, 3, 1, 2)[None]
    last2 = lambda g: g.reshape(bs, t, D_FF)[:, t - 2:]
    return (y_prompt.reshape(b, s, d), y_sample.reshape(bs, t, d),
            heads_p(ka_t), heads_p(va_t), ret_p[None], c_kv_p[None], k_rope_t_p.transpose(0, 2, 1)[None],
            jnp.stack([tails0[:, 6:8], tails1[:, 6:8]]),
            heads_s(new_k_s), heads_s(new_v_s), ret_s[None], c_kv_s.reshape(1, bs, t, C_KV_LORA),
            k_rope_t_s.reshape(C_ROPE, bs, t).transpose(1, 2, 0)[None], jnp.stack([last2(gate0), last2(gate1)]))
```

```python
import functools

import jax
import jax.numpy as jnp
from jax import lax
from jax.experimental import pallas as pl
from jax.experimental.pallas import tpu as pltpu

F32 = jnp.float32
BF16 = jnp.bfloat16

D_MODEL = 1024
A_HEADS = 8
A_HEAD_DIM = 64
A_WIDTH = A_HEADS * A_HEAD_DIM
A_PATTERNS = ((128, 1), (512, 4), (2048, 16))
BAND = 128
B_HEADS = 4
B_DIM = 128
B_WIDTH = B_HEADS * B_DIM
RET_CHUNK = 128
C_HEADS = 8
C_Q_LORA = 384
C_KV_LORA = 256
C_NOPE = 128
C_ROPE = 64
C_V_DIM = 128
C_CAT = C_KV_LORA + 128
MLA_SCALE = (C_NOPE + C_ROPE) ** -0.5
D_FF = 2816
ROPE_BASE = 10000.0
EPS = 1e-6
PAST_LEN = 8192
PAGE = 128

VMEM_LIMIT = 56 * 1024 * 1024

NT_DIMS = (((1,), (1,)), ((), ()))
TN_DIMS = (((0,), (0,)), ((), ()))


def _params(*sem):
    return pltpu.CompilerParams(dimension_semantics=sem, vmem_limit_bytes=VMEM_LIMIT)


def _resident(shape):
    nd = len(shape)
    return pl.BlockSpec(shape, lambda *_: (0,) * nd, pipeline_mode=pl.Buffered(1))


def _rms(x, g):
    return x * lax.rsqrt(jnp.mean(x * x, axis=-1, keepdims=True) + EPS) * g


def _dot(a, b):
    return jnp.dot(a, b, preferred_element_type=F32)


def _dot_nt(a, b):
    return lax.dot_general(a, b, NT_DIMS, preferred_element_type=F32)


def _norm_proj_kernel(x_ref, g_ref, w_ref, *out_refs, n_groups, transposed):
    h = _rms(x_ref[0], g_ref[...]).astype(BF16)
    t_refs = dict(zip(transposed, out_refs[n_groups:]))
    off = 0
    for grp, o_ref in enumerate(out_refs[:n_groups]):
        n = o_ref.shape[-1]
        r = _dot(h, w_ref[:, off:off + n])
        o_ref[0] = r
        if grp in t_refs:
            t_refs[grp][0] = r.T
        off += n


def _norm_proj(x, g, w, widths, tm, transposed=()):
    b, s, d = x.shape
    return pl.pallas_call(
        functools.partial(_norm_proj_kernel, n_groups=len(widths), transposed=tuple(transposed)),
        grid=(b, s // tm),
        in_specs=[pl.BlockSpec((1, tm, d), lambda i, j: (i, j, 0)),
                  _resident((1, d)),
                  _resident(w.shape)],
        out_specs=([pl.BlockSpec((1, tm, n), lambda i, j: (i, j, 0)) for n in widths]
                   + [pl.BlockSpec((1, widths[t], tm), lambda i, j: (i, 0, j)) for t in transposed]),
        out_shape=([jax.ShapeDtypeStruct((b, s, n), F32) for n in widths]
                   + [jax.ShapeDtypeStruct((b, widths[t], s), F32) for t in transposed]),
        compiler_params=_params("parallel", "parallel"),
        name="norm_proj",
    )(x, g.reshape(1, d), w)


def _unrolled_loop(n_items, fn):
    unroll = next(u for u in (16, 15, 12, 8, 6, 5, 4, 3, 2, 1) if n_items % u == 0)
    if n_items <= unroll:
        for t in range(n_items):
            fn(t)
        return

    def body(it, carry):
        for u in range(unroll):
            fn(it * unroll + u)
        return carry

    lax.fori_loop(0, n_items // unroll, body, 0)


def _attn_a_kernel(slopes_ref, q_ref, k_ref, v_ref, o_ref, os_ref, ls_ref, bias_ref, *, seq):
    pair = pl.program_id(1)
    is_a = lax.broadcasted_iota(jnp.int32, (1, 128), 1) < A_HEAD_DIM

    row = lax.broadcasted_iota(jnp.int32, (2 * BAND, 2 * BAND), 0)
    col = lax.broadcasted_iota(jnp.int32, (2 * BAND, 2 * BAND), 1)
    rel = BAND + (row & (BAND - 1)) - col
    slope = jnp.where(row < BAND, slopes_ref[2 * pair], slopes_ref[2 * pair + 1])
    in_band = (rel >= 0) & (rel <= BAND)
    for pat, (_, dil) in enumerate(A_PATTERNS):
        bias_ref[pat] = jnp.where(in_band, -slope * (rel * dil).astype(F32), -jnp.inf)

    def block(pat, dil, q_start, k_start, first):
        nk = BAND if first else 2 * BAND
        q = q_ref[0, pl.ds(q_start, BAND, stride=dil), :] * (A_HEAD_DIM ** -0.5)
        k = k_ref[0, pl.ds(k_start, nk, stride=dil), :].astype(BF16)
        v = v_ref[0, pl.ds(k_start, nk, stride=dil), :].astype(BF16)
        q2 = jnp.concatenate([jnp.where(is_a, q, 0.0), jnp.where(is_a, 0.0, q)], axis=0).astype(BF16)
        s = _dot_nt(q2, k) + (bias_ref[pat, :, BAND:] if first else bias_ref[pat])
        m = jnp.max(s, axis=-1, keepdims=True)
        e = jnp.exp(s - m)
        l = jnp.sum(e, axis=-1, keepdims=True)
        o = _dot(e.astype(BF16), v) / l
        lse = m + jnp.log(l)
        os_ref[pat, pl.ds(q_start, BAND, stride=dil), :] = jnp.where(is_a, o[:BAND], o[BAND:])
        ls_ref[pat, pl.ds(q_start, BAND, stride=dil), :] = jnp.where(is_a, lse[:BAND], lse[BAND:])

    for pat, (_, dil) in enumerate(A_PATTERNS):
        nb = seq // dil // BAND

        def first_block(r, pat=pat, dil=dil):
            block(pat, dil, r, r, True)

        def later_block(t, pat=pat, dil=dil, nb=nb):
            r = t // (nb - 1)
            j = 1 + t % (nb - 1)
            block(pat, dil, r + dil * BAND * j, r + dil * BAND * (j - 1), False)

        _unrolled_loop(dil, first_block)
        if nb > 1:
            _unrolled_loop(dil * (nb - 1), later_block)

    rows = 256

    def combine(i, carry):
        sl = pl.ds(pl.multiple_of(i * rows, rows), rows)
        l0, l1, l2 = ls_ref[0, sl, :], ls_ref[1, sl, :], ls_ref[2, sl, :]
        mx = jnp.maximum(jnp.maximum(l0, l1), l2)
        w0, w1, w2 = jnp.exp(l0 - mx), jnp.exp(l1 - mx), jnp.exp(l2 - mx)
        num = w0 * os_ref[0, sl, :] + w1 * os_ref[1, sl, :] + w2 * os_ref[2, sl, :]
        o_ref[0, sl, :] = (num / (w0 + w1 + w2)).astype(o_ref.dtype)
        return carry

    lax.fori_loop(0, seq // rows, combine, 0)


def _attn_a_prompt(q, k, v, slopes):
    b, s, _ = q.shape
    assert s % (A_PATTERNS[-1][1] * BAND) == 0
    spec = pl.BlockSpec((1, s, 128), lambda i, p, *_: (i, 0, p))
    return pl.pallas_call(
        functools.partial(_attn_a_kernel, seq=s),
        grid_spec=pltpu.PrefetchScalarGridSpec(
            num_scalar_prefetch=1,
            grid=(b, A_WIDTH // 128),
            in_specs=[spec, spec, spec],
            out_specs=spec,
            scratch_shapes=[pltpu.VMEM((3, s, 128), F32), pltpu.VMEM((3, s, 128), F32),
                            pltpu.VMEM((3, 2 * BAND, 2 * BAND), F32)]),
        out_shape=jax.ShapeDtypeStruct((b, s, A_WIDTH), BF16),
        compiler_params=_params("parallel", "parallel"),
        name="attn_a_prompt",
    )(slopes, q, k, v)


def _pattern_multiplicity(delta):
    mult = jnp.zeros(delta.shape, F32)
    for window, dil in A_PATTERNS:
        hit = (delta >= 0) & (delta <= window)
        if dil > 1:
            hit = hit & ((delta & (dil - 1)) == 0)
        mult = mult + jnp.where(hit, 1.0, 0.0)
    return mult


def _attn_a_dec_body(slopes_ref, half, seq_in_group, q_ref, kn_ref, vn_ref, knt_ref, vnt_ref, ck_ref, cv_ref,
                     o_ref, nk_ref, nv_ref, *, nbuf, t_new):
    width = q_ref.shape[-1]
    heads = width // A_HEAD_DIM
    rows = heads * t_new
    q = q_ref[0]
    q_rows = jnp.concatenate([q] * heads, axis=0)
    row = lax.broadcasted_iota(jnp.int32, (rows, width), 0)
    lane = lax.broadcasted_iota(jnp.int32, (rows, width), 1)
    qm = jnp.where(row // t_new == lane // A_HEAD_DIM, q_rows, 0.0).astype(BF16)

    rcol = lax.broadcasted_iota(jnp.int32, (rows, 1), 0)
    slope = jnp.zeros((rows, 1), F32)
    for h in range(heads):
        slope = jnp.where(rcol // t_new == h, slopes_ref[half * heads + h], slope)
    tok = rcol % t_new

    s_c = _dot(qm, ck_ref[0].astype(BF16)) * (A_HEAD_DIM ** -0.5)
    pos = lax.broadcasted_iota(jnp.int32, (rows, nbuf), 1)
    delta_c = nbuf + tok - pos
    mult_c = _pattern_multiplicity(delta_c)
    s_c = jnp.where(mult_c > 0, s_c - slope * delta_c.astype(F32), -jnp.inf)
    qf = qm.astype(F32)
    knf = kn_ref[0].astype(BF16).astype(F32)
    vnf = vn_ref[0].astype(BF16).astype(F32)
    s_n, mult_n = [], []
    for t in range(t_new):
        delta = tok - t
        mult = _pattern_multiplicity(delta)
        sc = jnp.sum(qf * knf[t:t + 1, :], axis=-1, keepdims=True) * (A_HEAD_DIM ** -0.5)
        s_n.append(jnp.where(mult > 0, sc - slope * delta.astype(F32), -jnp.inf))
        mult_n.append(mult)
    m = jnp.max(s_c, axis=-1, keepdims=True)
    for sc in s_n:
        m = jnp.maximum(m, sc)
    e_c = jnp.exp(s_c - m) * mult_c
    l = jnp.sum(e_c, axis=-1, keepdims=True)
    acc = _dot_nt(e_c.astype(BF16), cv_ref[0].astype(BF16))
    for t in range(t_new):
        e = jnp.exp(s_n[t] - m) * mult_n[t]
        l = l + e
        acc = acc + e * vnf[t:t + 1, :]
    res = acc / l
    lane_head = lax.broadcasted_iota(jnp.int32, (t_new, width), 1) // A_HEAD_DIM
    out = jnp.zeros((t_new, width), F32)
    for h in range(heads):
        out = jnp.where(lane_head == h, res[h * t_new:(h + 1) * t_new], out)
    o_ref[0] = out.astype(o_ref.dtype)

    is_new = lax.broadcasted_iota(jnp.int32, (width, 128), 1) >= 128 - t_new
    to_tail = (128 - t_new - seq_in_group * t_new) % 128
    for c_ref, nt_ref, n_ref in ((ck_ref, knt_ref, nk_ref), (cv_ref, vnt_ref, nv_ref)):
        rolled = pltpu.roll(c_ref[0], nbuf - t_new, 1)
        n_ref[0, :, 0:nbuf - 128] = rolled[:, 0:nbuf - 128]
        n_ref[0, :, nbuf - 128:nbuf] = jnp.where(is_new, pltpu.roll(nt_ref[...], to_tail, 1),
                                                 rolled[:, nbuf - 128:nbuf])


def _retention_head(q, k, v, gate, rope_q, rope_k, dmask, qdec, kdec, cdec, g, state):
    (cos_q, sin_q), (cos_k, sin_k) = rope_q, rope_k
    qr = (q * cos_q + pltpu.roll(q, B_DIM // 2, 1) * sin_q).astype(BF16)
    kr = (k * cos_k + pltpu.roll(k, B_DIM // 2, 1) * sin_k) * (B_DIM ** -0.5)
    v = v.astype(BF16)
    att = _dot_nt(qr, kr.astype(BF16)) * dmask
    o = _dot(att.astype(BF16), v) + _dot(qr, state.astype(BF16)) * qdec
    kd = (kr * kdec).astype(BF16)
    new_state = state * cdec + lax.dot_general(kd, v, TN_DIMS, preferred_element_type=F32)
    d = o - jnp.mean(o, axis=-1, keepdims=True)
    y = d * lax.rsqrt(jnp.mean(d * d, axis=-1, keepdims=True) + EPS) * g
    return y * jax.nn.silu(gate), new_state


def _retention_kernel(q_ref, k_ref, v_ref, gate_ref, cos_ref, sin_ref, dmask_ref, qdec_ref, kdec_ref,
                      cdec_ref, g_ref, s0_ref, o_ref, s_ref, *, chunk, n_chunks):
    @pl.when(pl.program_id(1) == 0)
    def _():
        s_ref[...] = s0_ref[...]

    for c in range(n_chunks):
        rows = slice(c * chunk, (c + 1) * chunk)
        rope = (cos_ref[rows, :], sin_ref[rows, :])
        for h in range(B_HEADS):
            cols = slice(h * B_DIM, (h + 1) * B_DIM)
            y, s_ref[0, h] = _retention_head(
                q_ref[0, rows, cols], k_ref[0, rows, cols], v_ref[0, rows, cols], gate_ref[0, rows, cols],
                rope, rope, dmask_ref[h], qdec_ref[h], kdec_ref[h], cdec_ref[h], g_ref[:, cols], s_ref[0, h])
            o_ref[0, rows, cols] = y.astype(o_ref.dtype)


def _retention_dec_kernel(q_ref, k_ref, v_ref, gate_ref, cosq_ref, sinq_ref, cosk_ref, sink_ref, dmask_ref,
                          qdec_ref, kdec_ref, cdec_ref, g_ref, s0_ref, o_ref, s_ref, kpad, vpad, *, n_seq):
    t_rows = k_ref.shape[1]
    kpad[...] = jnp.zeros(kpad.shape, F32)
    vpad[...] = jnp.zeros(vpad.shape, F32)
    rope_q = (cosq_ref[...], sinq_ref[...])
    rope_k = (cosk_ref[...], sink_ref[...])
    for b in range(n_seq):
        kpad[b, 0:t_rows, :] = k_ref[b]
        vpad[b, 0:t_rows, :] = v_ref[b]
        for h in range(B_HEADS):
            cols = slice(h * B_DIM, (h + 1) * B_DIM)
            y, s_ref[b, h] = _retention_head(
                q_ref[b, :, cols], kpad[b, :, cols], vpad[b, :, cols], gate_ref[b, :, cols],
                rope_q, rope_k, dmask_ref[h], qdec_ref[h], kdec_ref[h], cdec_ref[h], g_ref[:, cols], s0_ref[b, h])
            o_ref[b, :, cols] = y.astype(o_ref.dtype)


def _retention_tables(chunk, q_pad, k_pad):
    lg = jnp.log(1.0 - 2.0 ** (-5.0 - jnp.arange(B_HEADS, dtype=F32)))
    qi = jnp.arange(q_pad, dtype=F32)
    ki = jnp.arange(k_pad, dtype=F32)
    rel = qi[:, None] - ki[None, :]
    live = (rel >= 0) & (qi < chunk)[:, None] & (ki < chunk)[None, :]
    dmask = jnp.where(live, jnp.exp(jnp.maximum(rel, 0.0)[None] * lg[:, None, None]), 0.0)
    qdec = jnp.where((qi < chunk)[None, :], jnp.exp((qi + 1.0)[None, :] * lg[:, None]), 0.0)
    kdec = jnp.where((ki < chunk)[None, :], jnp.exp((chunk - 1.0 - ki)[None, :] * lg[:, None]), 0.0)
    cdec = jnp.exp(chunk * lg)
    bc = lambda a: jnp.broadcast_to(a[..., None], a.shape + (B_DIM,))
    return dmask, bc(qdec), bc(kdec), bc(cdec[:, None])


def _rope_tables(pos, dim, width):
    half = dim // 2
    inv = ROPE_BASE ** (-jnp.arange(half, dtype=F32) / half)
    ang = pos.astype(F32)[:, None] * inv[None, :]
    cos, sin = jnp.cos(ang), jnp.sin(ang)
    zeros = jnp.zeros((pos.shape[0], width - dim), F32)
    return (jnp.concatenate([cos, cos, zeros], axis=1), jnp.concatenate([-sin, sin, zeros], axis=1))


def _retention_decode(q, k, v, gate, pos, g_ret, s0, n_seq):
    b, t, _ = q.shape
    t_pad = 8
    assert t <= t_pad and b % n_seq == 0
    pad_rows = lambda a: jnp.pad(a, ((0, 0), (0, t_pad - t), (0, 0)))
    cosk, sink = _rope_tables(jnp.pad(pos, (0, RET_CHUNK - t)), B_DIM, B_DIM)
    dmask, qdec, kdec, cdec = _retention_tables(t, t_pad, RET_CHUNK)
    tok = pl.BlockSpec((n_seq, t_pad, B_WIDTH), lambda i: (i, 0, 0))
    st = pl.BlockSpec((n_seq, B_HEADS, B_DIM, B_DIM), lambda i: (i, 0, 0, 0))
    consts = (cosk[:t_pad], sink[:t_pad], cosk, sink, dmask, qdec, kdec, cdec, g_ret.reshape(1, B_WIDTH))
    o, s_new = pl.pallas_call(
        functools.partial(_retention_dec_kernel, n_seq=n_seq),
        grid=(b // n_seq,),
        in_specs=[tok, tok, tok, tok] + [_resident(c.shape) for c in consts] + [st],
        out_specs=[tok, st],
        out_shape=[jax.ShapeDtypeStruct((b, t_pad, B_WIDTH), BF16),
                   jax.ShapeDtypeStruct((b, B_HEADS, B_DIM, B_DIM), F32)],
        scratch_shapes=[pltpu.VMEM((n_seq, RET_CHUNK, B_WIDTH), F32),
                        pltpu.VMEM((n_seq, RET_CHUNK, B_WIDTH), F32)],
        compiler_params=_params("parallel"),
        name="retention_decode",
    )(pad_rows(q), pad_rows(k), pad_rows(v), pad_rows(gate), *consts, s0)
    return o[:, :t], s_new


def _retention(q, k, v, gate, pos, g_ret, s0, rows):
    b, s, _ = q.shape
    cos, sin = _rope_tables(pos, B_DIM, B_DIM)
    dmask, qdec, kdec, cdec = _retention_tables(RET_CHUNK, RET_CHUNK, RET_CHUNK)
    tok = pl.BlockSpec((1, rows, B_WIDTH), lambda i, j: (i, j, 0))
    tab = pl.BlockSpec((rows, B_DIM), lambda i, j: (j, 0))
    st = pl.BlockSpec((1, B_HEADS, B_DIM, B_DIM), lambda i, j: (i, 0, 0, 0))
    return pl.pallas_call(
        functools.partial(_retention_kernel, chunk=RET_CHUNK, n_chunks=rows // RET_CHUNK),
        grid=(b, s // rows),
        in_specs=[tok, tok, tok, tok, tab, tab, _resident(dmask.shape), _resident(qdec.shape),
                  _resident(kdec.shape), _resident(cdec.shape), _resident((1, B_WIDTH)), st],
        out_specs=[tok, st],
        out_shape=[jax.ShapeDtypeStruct((b, s, B_WIDTH), BF16),
                   jax.ShapeDtypeStruct((b, B_HEADS, B_DIM, B_DIM), F32)],
        compiler_params=_params("parallel", "arbitrary"),
        name="retention",
    )(q, k, v, gate, cos, sin, dmask, qdec, kdec, cdec, g_ret.reshape(1, B_WIDTH), s0)


def _out_proj_kernel(a_ref, b_ref, w_ref, x_ref, y_ref):
    na = a_ref.shape[-1]
    y_ref[...] = x_ref[...] + _dot(a_ref[...], w_ref[:na, :]) + _dot(b_ref[...], w_ref[na:, :])


def _out_proj(a, b, w, x, tm):
    m, d = x.shape
    return pl.pallas_call(
        _out_proj_kernel,
        grid=(m // tm,),
        in_specs=[pl.BlockSpec((tm, a.shape[1]), lambda i: (i, 0)),
                  pl.BlockSpec((tm, b.shape[1]), lambda i: (i, 0)),
                  _resident(w.shape),
                  pl.BlockSpec((tm, d), lambda i: (i, 0))],
        out_specs=pl.BlockSpec((tm, d), lambda i: (i, 0)),
        out_shape=jax.ShapeDtypeStruct((m, d), F32),
        compiler_params=_params("parallel"),
        name="out_proj",
    )(a, b, w, x)


def _rope_group(x, cos, sin):
    return x * cos + (pltpu.roll(x, C_ROPE // 2, 1) + pltpu.roll(x, 128 - C_ROPE // 2, 1)) * sin


def _mla_prep_kernel(x_ref, g_ref, wdq_ref, gq_ref, wuqn_ref, wuqr_ref, wdkv_ref, gkv_ref, wukt_ref,
                     cos_ref, sin_ref, qcat_ref, kcat_ref, ckv_ref, kr_ref):
    h = _rms(x_ref[0], g_ref[...]).astype(BF16)
    cq = _rms(_dot(h, wdq_ref[...]), gq_ref[...]).astype(BF16)
    q_nope = _dot(cq, wuqn_ref[...])
    q_rope = _dot(cq, wuqr_ref[...])
    kv = _dot(h, wdkv_ref[...])
    cos = cos_ref[...]
    sin = sin_ref[...]
    c_kv = _rms(kv[:, :C_KV_LORA], gkv_ref[...])
    k_rope = _rope_group(kv[:, C_KV_LORA:], cos, sin)
    ckv_ref[0] = c_kv
    kr_ref[0] = k_rope.T[:C_ROPE, :]
    kcat_ref[0, :, :C_KV_LORA] = c_kv.astype(BF16)
    kcat_ref[0, :, C_KV_LORA:] = k_rope.astype(BF16)
    for hd in range(C_HEADS):
        cols = slice(hd * 128, (hd + 1) * 128)
        qcat_ref[0, hd, :, :C_KV_LORA] = _dot(q_nope[:, cols].astype(BF16), wukt_ref[hd]).astype(BF16)
        qcat_ref[0, hd, :, C_KV_LORA:] = _rope_group(q_rope[:, cols], cos, sin).astype(BF16)


def _mla_prep(x, pos, g, wdq, gq, wuqn, wuqr, wdkv, gkv, wukt, tm):
    b, s, d = x.shape
    cos, sin = _rope_tables(pos, C_ROPE, 128)
    tab = pl.BlockSpec((tm, 128), lambda i, j: (j, 0))
    return pl.pallas_call(
        _mla_prep_kernel,
        grid=(b, s // tm),
        in_specs=[pl.BlockSpec((1, tm, d), lambda i, j: (i, j, 0)),
                  _resident((1, d)), _resident(wdq.shape), _resident((1, C_Q_LORA)),
                  _resident(wuqn.shape), _resident(wuqr.shape), _resident(wdkv.shape),
                  _resident((1, C_KV_LORA)), _resident(wukt.shape), tab, tab],
        out_specs=[pl.BlockSpec((1, C_HEADS, tm, C_CAT), lambda i, j: (i, 0, j, 0)),
                   pl.BlockSpec((1, tm, C_CAT), lambda i, j: (i, j, 0)),
                   pl.BlockSpec((1, tm, C_KV_LORA), lambda i, j: (i, j, 0)),
                   pl.BlockSpec((1, C_ROPE, tm), lambda i, j: (i, 0, j))],
        out_shape=[jax.ShapeDtypeStruct((b, C_HEADS, s, C_CAT), BF16),
                   jax.ShapeDtypeStruct((b, s, C_CAT), BF16),
                   jax.ShapeDtypeStruct((b, s, C_KV_LORA), F32),
                   jax.ShapeDtypeStruct((b, C_ROPE, s), F32)],
        compiler_params=_params("parallel", "parallel"),
        name="mla_prep",
    )(x, g.reshape(1, d), wdq, gq.reshape(1, -1), wuqn, wuqr, wdkv, gkv.reshape(1, -1), wukt, cos, sin)


def _mla_out(o_lat_heads, wuv_ref, wo_ref, x):
    o = jnp.concatenate([_dot(o_lat_heads[hd].astype(BF16), wuv_ref[hd]) for hd in range(C_HEADS)], axis=1)
    return x + _dot(o.astype(BF16), wo_ref[...])


def _mla_flash_blocks(i, q_ref, k_ref, m_ref, l_ref, acc_ref, *, tq, tk):
    last = (i * tq + tq - 1) // tk
    c_exp = MLA_SCALE * 1.4426950408889634
    rows = 512
    group = rows // tq

    def key_block(j, masked, width=tk, first=False):
        k = k_ref[0, pl.ds(pl.multiple_of(j * tk, tk), width), :]
        n_chunks = width // 128
        if masked:
            qpos = i * tq + lax.broadcasted_iota(jnp.int32, (rows, 128), 0) % tq
            lane = lax.broadcasted_iota(jnp.int32, (rows, 128), 1)
        for g in range(C_HEADS // group):
            hs = slice(g * group, (g + 1) * group)
            s = _dot_nt(q_ref[0, hs].reshape(rows, C_CAT), k)
            chunks = [s[:, c * 128:(c + 1) * 128] for c in range(n_chunks)]
            if masked:
                chunks = [jnp.where(j * tk + c * 128 + lane <= qpos, sc, -jnp.inf)
                          for c, sc in enumerate(chunks)]
            row_max = jnp.max(functools.reduce(jnp.maximum, chunks), axis=-1, keepdims=True)
            if first:
                m_new = jnp.broadcast_to(row_max, (rows, 128))
            else:
                m_old = m_ref[hs].reshape(rows, 128)
                m_new = jnp.maximum(m_old, row_max)
                alpha = jnp.exp2((m_old - m_new) * c_exp)
            p = [jnp.exp2((sc - m_new) * c_exp) for sc in chunks]
            row_sum = jnp.sum(functools.reduce(jnp.add, p), axis=-1, keepdims=True)
            pv = _dot(jnp.concatenate(p, axis=1).astype(BF16), k[:, :C_KV_LORA])
            if first:
                l_new = jnp.broadcast_to(row_sum, (rows, 128))
                acc = pv
            else:
                l_new = alpha * l_ref[hs].reshape(rows, 128) + row_sum
                acc = (jnp.concatenate([alpha] * (C_KV_LORA // 128), axis=1)
                       * acc_ref[hs].reshape(rows, C_KV_LORA) + pv)
            m_ref[hs] = m_new.reshape(group, tq, 128)
            l_ref[hs] = l_new.reshape(group, tq, 128)
            acc_ref[hs] = acc.reshape(group, tq, C_KV_LORA)

    def body(j, carry):
        key_block(j, masked=False)
        return carry

    @pl.when(last > 0)
    def _():
        key_block(0, masked=False, first=True)
        lax.fori_loop(1, last, body, 0)

    ratio = tk // tq
    for r in range(ratio):
        @pl.when(i == r)
        def _(r=r):
            key_block(0, masked=True, width=(r + 1) * tq, first=True)

        @pl.when((i % ratio == r) & (i >= ratio))
        def _(r=r):
            key_block(last, masked=True, width=(r + 1) * tq)


def _mla_flash_finish(x_ref, wuv_ref, wo_ref, y_ref, l_ref, acc_ref):
    heads = []
    for hd in range(C_HEADS):
        inv = 1.0 / l_ref[hd]
        heads.append(acc_ref[hd] * jnp.concatenate([inv] * (C_KV_LORA // 128), axis=1))
    y_ref[0] = _mla_out(heads, wuv_ref, wo_ref, x_ref[0])


def _flash_decode_kernel(slopes_ref, q_ref, k_ref, x_ref, wuv_ref, wo_ref,
                         dq_ref, dkn_ref, dvn_ref, dknt_ref, dvnt_ref, dck_ref, dcv_ref,
                         y_ref, do_ref, dnk_ref, dnv_ref, m_ref, l_ref, acc_ref,
                         *, tq, tk, nbuf, t_new):
    i = pl.program_id(1)
    step = pl.program_id(0) * pl.num_programs(1) + i
    _mla_flash_blocks(i, q_ref, k_ref, m_ref, l_ref, acc_ref, tq=tq, tk=tk)
    _attn_a_dec_body(slopes_ref, step % 2, (step // 2) % (128 // t_new), dq_ref, dkn_ref, dvn_ref, dknt_ref,
                     dvnt_ref, dck_ref, dcv_ref, do_ref, dnk_ref, dnv_ref, nbuf=nbuf, t_new=t_new)
    _mla_flash_finish(x_ref, wuv_ref, wo_ref, y_ref, l_ref, acc_ref)


def _mla_flash_with_decode(qcat, kcat, x, wuv, wo, slopes, dq, dkn, dvn, dkn_t, dvn_t, cache_kt, cache_vt,
                           tq, tk):
    b, s, d = x.shape
    bs, t_new, _ = dq.shape
    nbuf = cache_kt.shape[2]
    nq = s // tq
    width = A_WIDTH // 2
    assert b * nq == 2 * bs and 128 % t_new == 0
    per_group = 128 // t_new
    seq = lambda i, j: (i * nq + j) // 2
    half = lambda i, j: (i * nq + j) % 2
    small = pl.BlockSpec((1, t_new, width), lambda i, j, *_: (seq(i, j), 0, half(i, j)))
    newt = pl.BlockSpec((width, 128), lambda i, j, *_: (half(i, j), seq(i, j) // per_group))
    big = pl.BlockSpec((1, width, nbuf), lambda i, j, *_: (seq(i, j), half(i, j), 0))
    return pl.pallas_call(
        functools.partial(_flash_decode_kernel, tq=tq, tk=tk, nbuf=nbuf, t_new=t_new),
        grid_spec=pltpu.PrefetchScalarGridSpec(
            num_scalar_prefetch=1,
            grid=(b, nq),
            in_specs=[pl.BlockSpec((1, C_HEADS, tq, C_CAT), lambda i, j, *_: (i, 0, j, 0)),
                      pl.BlockSpec((1, s, C_CAT), lambda i, j, *_: (i, 0, 0)),
                      pl.BlockSpec((1, tq, d), lambda i, j, *_: (i, j, 0)),
                      _resident(wuv.shape), _resident(wo.shape),
                      small, small, small, newt, newt, big, big],
            out_specs=[pl.BlockSpec((1, tq, d), lambda i, j, *_: (i, j, 0)), small, big, big],
            scratch_shapes=[pltpu.VMEM((C_HEADS, tq, 128), F32), pltpu.VMEM((C_HEADS, tq, 128), F32),
                            pltpu.VMEM((C_HEADS, tq, C_KV_LORA), F32)]),
        out_shape=[jax.ShapeDtypeStruct((b, s, d), F32),
                   jax.ShapeDtypeStruct((bs, t_new, A_WIDTH), BF16),
                   jax.ShapeDtypeStruct(cache_kt.shape, F32),
                   jax.ShapeDtypeStruct(cache_vt.shape, F32)],
        compiler_params=_params("parallel", "parallel"),
        name="mla_flash_with_decode",
    )(slopes, qcat, kcat, x, wuv, wo, dq, dkn, dvn, dkn_t, dvn_t, cache_kt, cache_vt)


def _mla_dec_kernel(pt_ref, q_ref, kn_ref, cpool_ref, rpool_ref, o_ref, cbuf, rbuf, sem,
                    *, n_pages, t_new, n_split):
    b = pl.program_id(0)
    slot = b % 2
    per_split = n_pages // n_split

    def page_copies(batch, sl, pg):
        pid = pt_ref[pg, batch]
        return (pltpu.make_async_copy(cpool_ref.at[pid], cbuf.at[sl, pg], sem.at[0, sl]),
                pltpu.make_async_copy(rpool_ref.at[pid], rbuf.at[sl, :, pg * PAGE:(pg + 1) * PAGE],
                                      sem.at[1, sl]))

    def start_batch(batch, sl):
        for pg in range(n_pages):
            for cp in page_copies(batch, sl, pg):
                cp.start()

    @pl.when(b == 0)
    def _():
        start_batch(0, 0)

    @pl.when(b + 1 < pl.num_programs(0))
    def _():
        start_batch(b + 1, 1 - slot)

    q = q_ref[0]
    rows = q.shape[0]
    q_lat = q[:, :C_KV_LORA]
    q_rope = q[:, C_KV_LORA:C_KV_LORA + C_ROPE]

    for pg in range(n_pages):
        for cp in page_copies(b, slot, pg):
            cp.wait()

    parts = []
    for part in range(n_split):
        keys = per_split * PAGE
        c_part = cbuf[slot, part * per_split:(part + 1) * per_split].reshape(keys, C_KV_LORA).astype(BF16)
        r_part = rbuf[slot, :, part * keys:(part + 1) * keys].astype(BF16)
        s = (_dot_nt(q_lat, c_part) + _dot(q_rope, r_part)) * MLA_SCALE
        m = jnp.max(s, axis=-1, keepdims=True)
        p = jnp.exp(s - m)
        parts.append((m, jnp.sum(p, axis=-1, keepdims=True), _dot(p.astype(BF16), c_part)))

    qf = q.astype(F32)
    knf = kn_ref[0].astype(F32)
    tok = lax.broadcasted_iota(jnp.int32, (rows, 1), 0) % t_new
    s_n = []
    for t in range(t_new):
        sc = jnp.sum(qf * knf[t:t + 1, :], axis=-1, keepdims=True) * MLA_SCALE
        s_n.append(jnp.where(tok >= t, sc, -jnp.inf))

    m = functools.reduce(jnp.maximum, [pm for pm, _, _ in parts] + s_n)
    l = jnp.zeros((rows, 1), F32)
    acc = jnp.zeros((rows, C_KV_LORA), F32)
    for part_m, part_l, part_acc in parts:
        w = jnp.exp(part_m - m)
        l = l + w * part_l
        acc = acc + w * part_acc
    for t in range(t_new):
        e = jnp.exp(s_n[t] - m)
        l = l + e
        acc = acc + e * knf[t:t + 1, :C_KV_LORA]
    o_ref[0] = (acc / l).astype(o_ref.dtype)


def _mla_decode(q_rows, k_new, c_pool, r_pool_t, page_table_t):
    b, rows, _ = q_rows.shape
    t_new = k_new.shape[1]
    n_pages = page_table_t.shape[0]
    n_split = 4 if n_pages % 4 == 0 else 1
    return pl.pallas_call(
        functools.partial(_mla_dec_kernel, n_pages=n_pages, t_new=t_new, n_split=n_split),
        grid_spec=pltpu.PrefetchScalarGridSpec(
            num_scalar_prefetch=1,
            grid=(b,),
            in_specs=[pl.BlockSpec((1, rows, C_CAT), lambda i, *_: (i, 0, 0)),
                      pl.BlockSpec((1, t_new, C_CAT), lambda i, *_: (i, 0, 0)),
                      pl.BlockSpec(memory_space=pl.ANY),
                      pl.BlockSpec(memory_space=pl.ANY)],
            out_specs=pl.BlockSpec((1, rows, C_KV_LORA), lambda i, *_: (i, 0, 0)),
            scratch_shapes=[pltpu.VMEM((2, n_pages, PAGE, C_KV_LORA), F32),
                            pltpu.VMEM((2, C_ROPE, n_pages * PAGE), F32),
                            pltpu.SemaphoreType.DMA((2, 2))]),
        out_shape=jax.ShapeDtypeStruct((b, rows, C_KV_LORA), BF16),
        compiler_params=_params("arbitrary"),
        name="mla_decode",
    )(page_table_t, q_rows, k_new, c_pool, r_pool_t)


def _mla_out_kernel(o_ref, wuv_ref, wo_ref, x_ref, y_ref):
    y_ref[...] = _mla_out([o_ref[hd] for hd in range(C_HEADS)], wuv_ref, wo_ref, x_ref[...])


def _mla_out_proj(o_heads, wuv, wo, x):
    m, d = x.shape
    return pl.pallas_call(
        _mla_out_kernel,
        grid=(1,),
        in_specs=[_resident(o_heads.shape), _resident(wuv.shape), _resident(wo.shape), _resident(x.shape)],
        out_specs=pl.BlockSpec((m, d), lambda i: (0, 0)),
        out_shape=jax.ShapeDtypeStruct((m, d), F32),
        compiler_params=_params("arbitrary"),
        name="mla_out_proj",
    )(o_heads, wuv, wo, x)


def _ffn_kernel(*refs, tm, tiles_per_seq, seq_in_tile, final_norm):
    x_ref, g_ref, wup_ref, wconv_ref, bconv_ref, wdown_ref = refs[:6]
    refs = refs[6:]
    if seq_in_tile:
        e1_ref, e2_ref = refs[:2]
        refs = refs[2:]
    if final_norm:
        gfin_ref = refs[0]
        refs = refs[1:]
    y_ref, gate_out_ref, gbuf = refs
    halo = 8

    if seq_in_tile:
        gbuf[0:halo, :] = jnp.zeros((halo, D_FF), F32)
    else:
        @pl.when(pl.program_id(0) % tiles_per_seq == 0)
        def _():
            gbuf[0:halo, :] = jnp.zeros((halo, D_FF), F32)

    x = x_ref[...]
    h = _rms(x, g_ref[...]).astype(BF16)
    gate = _dot(h, wup_ref[:, :D_FF])
    up = _dot(h, wup_ref[:, D_FF:])
    gbuf[halo:halo + tm, :] = gate
    tap0 = gbuf[halo - 2:halo - 2 + tm, :]
    tap1 = gbuf[halo - 1:halo - 1 + tm, :]
    if seq_in_tile:
        tok = lax.broadcasted_iota(jnp.int32, (tm, 1), 0) % seq_in_tile
        tap0 = jnp.where(tok < 2, e2_ref[...], tap0)
        tap1 = jnp.where(tok < 1, e1_ref[...], tap1)
        gate_out_ref[...] = gate
    else:
        gate_out_ref[0] = gate[tm - halo:, :]
        gbuf[0:halo, :] = gate[tm - halo:, :]
    conv = tap0 * wconv_ref[0:1, :] + bconv_ref[...] + tap1 * wconv_ref[1:2, :] + gate * wconv_ref[2:3, :]
    act = (jax.nn.silu(conv) * up).astype(BF16)
    y = x + _dot(act, wdown_ref[...])
    if final_norm:
        y = _rms(y, gfin_ref[...])
    y_ref[...] = y


def _conv_ffn(x, g, wup, wconv, bconv, wdown, layer, *, tm, seq, conv_state=None, g_final=None):
    m, d = x.shape
    seq_in_tile = 0 if conv_state is None else seq
    row = pl.BlockSpec((tm, d), lambda i: (i, 0))
    wide = pl.BlockSpec((tm, D_FF), lambda i: (i, 0))
    of_layer = lambda a: pl.BlockSpec((None,) + a.shape[1:], lambda i: (layer, 0, 0), pipeline_mode=pl.Buffered(1))
    args = [x, g.reshape(1, d), wup, wconv, bconv.reshape(1, D_FF), wdown]
    in_specs = [row, _resident((1, d)), of_layer(wup), _resident(wconv.shape),
                _resident((1, D_FF)), of_layer(wdown)]
    if conv_state is not None:
        assert tm % seq == 0 and seq >= 2
        zeros = jnp.zeros((m // seq, seq - 2, D_FF), F32)
        e2 = jnp.concatenate([conv_state, zeros], axis=1).reshape(m, D_FF)
        e1 = jnp.concatenate([conv_state[:, 1:], zeros, zeros[:, :1]], axis=1).reshape(m, D_FF)
        args += [e1, e2]
        in_specs += [wide, wide]
        gate_spec = wide
        gate_shape = jax.ShapeDtypeStruct((m, D_FF), F32)
    else:
        assert seq % tm == 0
        gate_spec = pl.BlockSpec((1, 8, D_FF), lambda i: (i // (seq // tm), 0, 0))
        gate_shape = jax.ShapeDtypeStruct((m // seq, 8, D_FF), F32)
    if g_final is not None:
        args.append(g_final.reshape(1, d))
        in_specs.append(_resident((1, d)))
    return pl.pallas_call(
        functools.partial(_ffn_kernel, tm=tm, tiles_per_seq=max(seq // tm, 1), seq_in_tile=seq_in_tile,
                          final_norm=g_final is not None),
        grid=(m // tm,),
        in_specs=in_specs,
        out_specs=[row, gate_spec],
        out_shape=[jax.ShapeDtypeStruct((m, d), F32), gate_shape],
        scratch_shapes=[pltpu.VMEM((tm + 8, D_FF), F32)],
        compiler_params=_params("arbitrary"),
        name="conv_ffn",
    )(*args)


def _prepare_weights(w_in_ab, w_out_ab, w_dq, w_uq, w_dkv, w_uk, w_uv, w_o_c, w_up, w_down):
    per_head = C_NOPE + C_ROPE
    uq = w_uq[0].reshape(C_Q_LORA, C_HEADS, per_head)
    wuqr = jnp.pad(uq[:, :, C_NOPE:], ((0, 0), (0, 0), (0, 128 - C_ROPE)))
    return dict(
        w_in=w_in_ab[0].astype(BF16),
        w_out=w_out_ab[0].astype(BF16),
        wdq=w_dq[0].astype(BF16),
        wuqn=uq[:, :, :C_NOPE].reshape(C_Q_LORA, C_HEADS * C_NOPE).astype(BF16),
        wuqr=wuqr.reshape(C_Q_LORA, C_HEADS * 128).astype(BF16),
        wdkv=jnp.pad(w_dkv[0], ((0, 0), (0, 128 - C_ROPE))).astype(BF16),
        wukt=w_uk[0].transpose(1, 2, 0).astype(BF16),
        wuv=w_uv[0].transpose(1, 0, 2).astype(BF16),
        wo=w_o_c[0].astype(BF16),
        w_up=w_up.astype(BF16),
        w_down=w_down.astype(BF16),
    )


def _alibi_slopes():
    return 2.0 ** (-8.0 * (jnp.arange(A_HEADS, dtype=F32) + 1.0) / A_HEADS)


AB_WIDTHS = (A_WIDTH,) * 3 + (B_WIDTH,) * 4


def kernel(x_prompt, x_sample, cache_a_k, cache_a_v, state_ret, cache_c_kv, cache_k_rope, state_ffn_conv,
           page_table, g_mix, g_ffn, g_final, w_in_ab, w_out_ab, g_ret, w_dq, g_q, w_uq, w_dkv, g_kv,
           w_uk, w_uv, w_o_c, w_up, w_conv, b_conv, w_down):
    assert g_mix.shape[0] == 2 and w_in_ab.shape[0] == 1 and w_dq.shape[0] == 1
    w = _prepare_weights(w_in_ab, w_out_ab, w_dq, w_uq, w_dkv, w_uk, w_uv, w_o_c, w_up, w_down)
    slopes = _alibi_slopes()
    tm_ffn = 512

    def ffn(x, layer, **kw):
        return _conv_ffn(x, g_ffn[layer], w["w_up"], w_conv[layer], b_conv[layer], w["w_down"], layer, **kw)

    def mla_prep(x3, pos, tm):
        return _mla_prep(x3, pos, g_mix[1], w["wdq"], g_q[0], w["wuqn"], w["wuqr"], w["wdkv"], g_kv[0],
                         w["wukt"], tm=tm)

    b, s, d = x_prompt.shape
    m = b * s
    pos_p = jnp.arange(s)
    qa, ka, va, qb, kb, vb, gb, ka_t, va_t = _norm_proj(x_prompt, g_mix[0], w["w_in"], AB_WIDTHS, tm=512,
                                                        transposed=(1, 2))
    o_a = _attn_a_prompt(qa, ka, va, slopes)
    o_b, ret_p = _retention(qb, kb, vb, gb, pos_p, g_ret[0], jnp.zeros((b, B_HEADS, B_DIM, B_DIM), F32), rows=512)
    xp = _out_proj(o_a.reshape(m, -1), o_b.reshape(m, -1), w["w_out"], x_prompt.reshape(m, d), tm=1024)
    xp, tails0 = ffn(xp, 0, tm=tm_ffn, seq=s)
    qcat_p, kcat_p, c_kv_p, k_rope_t_p = mla_prep(xp.reshape(b, s, d), pos_p, 1024)

    bs, t, _ = x_sample.shape
    ms = bs * t
    pos_s = PAST_LEN + jnp.arange(t)
    xs = x_sample.reshape(ms, d)
    sq, sk, sv, sqb, skb, svb, sgb, sk_t, sv_t = _norm_proj(xs.reshape(1, ms, d), g_mix[0], w["w_in"], AB_WIDTHS,
                                                            tm=ms, transposed=(1, 2))
    per_seq = lambda a: a.reshape(bs, t, -1)
    nbuf = cache_a_k.shape[2]
    feature_major = lambda a: a[0].transpose(0, 2, 3, 1).reshape(bs, A_WIDTH, nbuf)

    xp, so_a, new_k_s, new_v_s = _mla_flash_with_decode(
        qcat_p, kcat_p, xp.reshape(b, s, d), w["wuv"], w["wo"], slopes, per_seq(sq), per_seq(sk), per_seq(sv),
        sk_t[0], sv_t[0], feature_major(cache_a_k), feature_major(cache_a_v), tq=128, tk=512)
    y_prompt, tails1 = ffn(xp.reshape(m, d), 1, tm=tm_ffn, seq=s, g_final=g_final)

    so_b, ret_s = _retention_decode(per_seq(sqb), per_seq(skb), per_seq(svb), per_seq(sgb), pos_s, g_ret[0],
                                    state_ret[0], n_seq=4)
    xs = _out_proj(so_a.reshape(ms, -1), so_b.reshape(ms, -1), w["w_out"], xs, tm=ms)
    xs, gate0 = ffn(xs, 0, tm=ms, seq=t, conv_state=state_ffn_conv[0])
    qcat_s, kcat_s, c_kv_s, k_rope_t_s = mla_prep(xs.reshape(1, ms, d), jnp.tile(pos_s, bs), ms)
    q_rows = qcat_s.reshape(C_HEADS, bs, t, C_CAT).transpose(1, 0, 2, 3).reshape(bs, C_HEADS * t, C_CAT)
    o_lat = _mla_decode(q_rows, kcat_s.reshape(bs, t, C_CAT), cache_c_kv[0], cache_k_rope[0].transpose(0, 2, 1),
                        page_table.T)
    o_heads = o_lat.reshape(bs, C_HEADS, t, C_KV_LORA).transpose(1, 0, 2, 3).reshape(C_HEADS, ms, C_KV_LORA)
    xs = _mla_out_proj(o_heads, w["wuv"], w["wo"], xs)
    y_sample, gate1 = ffn(xs, 1, tm=ms, seq=t, conv_state=state_ffn_conv[1], g_final=g_final)

    keep = min(A_PATTERNS[-1][0], s)
    heads_p = lambda a: a[:, :, s - keep:].reshape(b, A_HEADS, A_HEAD_DIM, keep).transpose(0, 3, 1, 2)[None]
    heads_s = lambda a: a.reshape(bs, A_HEADS, A_HEAD_DIM, nbuf).transpose(0, 3, 1, 2)[None]
    last2 = lambda g: g.reshape(bs, t, D_FF)[:, t - 2:]
    return (y_prompt.reshape(b, s, d), y_sample.reshape(bs, t, d),
            heads_p(ka_t), heads_p(va_t), ret_p[None], c_kv_p[None], k_rope_t_p.transpose(0, 2, 1)[None],
            jnp.stack([tails0[:, 6:8], tails1[:, 6:8]]),
            heads_s(new_k_s), heads_s(new_v_s), ret_s[None], c_kv_s.reshape(1, bs, t, C_KV_LORA),
            k_rope_t_s.reshape(C_ROPE, bs, t).transpose(1, 2, 0)[None], jnp.stack([last2(gate0), last2(gate1)]))
```

```python
import functools

import jax
import jax.numpy as jnp
import numpy as np
from jax import lax
from jax.experimental import pallas as pl
from jax.experimental.pallas import tpu as pltpu

F32 = jnp.float32
BF16 = jnp.bfloat16

D_MODEL = 1024
A_HEADS = 8
A_HEAD_DIM = 64
A_WIDTH = A_HEADS * A_HEAD_DIM
A_PATTERNS = ((128, 1), (512, 4), (2048, 16))
BAND = 128
B_HEADS = 4
B_DIM = 128
B_WIDTH = B_HEADS * B_DIM
RET_CHUNK = 128
C_HEADS = 8
C_Q_LORA = 384
C_KV_LORA = 256
C_NOPE = 128
C_ROPE = 64
C_V_DIM = 128
C_CAT = C_KV_LORA + 128
MLA_SCALE = (C_NOPE + C_ROPE) ** -0.5
D_FF = 2816
ROPE_BASE = 10000.0
EPS = 1e-6
PAST_LEN = 8192
PAGE = 128

VMEM_LIMIT = 56 * 1024 * 1024

NT_DIMS = (((1,), (1,)), ((), ()))
TN_DIMS = (((0,), (0,)), ((), ()))


def _params(*sem):
    return pltpu.CompilerParams(dimension_semantics=sem, vmem_limit_bytes=VMEM_LIMIT)


def _resident(shape):
    nd = len(shape)
    return pl.BlockSpec(shape, lambda *_: (0,) * nd, pipeline_mode=pl.Buffered(1))


def _rms(x, g):
    return x * lax.rsqrt(jnp.mean(x * x, axis=-1, keepdims=True) + EPS) * g


def _dot(a, b):
    return jnp.dot(a, b, preferred_element_type=F32)


def _dot_nt(a, b):
    return lax.dot_general(a, b, NT_DIMS, preferred_element_type=F32)


def _norm_proj_kernel(x_ref, g_ref, w_ref, *out_refs, n_groups, transposed):
    h = _rms(x_ref[0], g_ref[...]).astype(BF16)
    t_refs = dict(zip(transposed, out_refs[n_groups:]))
    off = 0
    for grp, o_ref in enumerate(out_refs[:n_groups]):
        n = o_ref.shape[-1]
        r = _dot(h, w_ref[:, off:off + n])
        o_ref[0] = r
        if grp in t_refs:
            t_refs[grp][0] = r.T
        off += n


def _norm_proj(x, g, w, widths, tm, transposed=()):
    b, s, d = x.shape
    return pl.pallas_call(
        functools.partial(_norm_proj_kernel, n_groups=len(widths), transposed=tuple(transposed)),
        grid=(b, s // tm),
        in_specs=[pl.BlockSpec((1, tm, d), lambda i, j: (i, j, 0)),
                  _resident((1, d)),
                  _resident(w.shape)],
        out_specs=([pl.BlockSpec((1, tm, n), lambda i, j: (i, j, 0)) for n in widths]
                   + [pl.BlockSpec((1, widths[t], tm), lambda i, j: (i, 0, j)) for t in transposed]),
        out_shape=([jax.ShapeDtypeStruct((b, s, n), F32) for n in widths]
                   + [jax.ShapeDtypeStruct((b, widths[t], s), F32) for t in transposed]),
        compiler_params=_params("parallel", "parallel"),
        name="norm_proj",
    )(x, g.reshape(1, d), w)


def _unrolled_loop(n_items, fn):
    unroll = next(u for u in (16, 15, 12, 8, 6, 5, 4, 3, 2, 1) if n_items % u == 0)
    if n_items <= unroll:
        for t in range(n_items):
            fn(t)
        return

    def body(it, carry):
        for u in range(unroll):
            fn(it * unroll + u)
        return carry

    lax.fori_loop(0, n_items // unroll, body, 0)


def _attn_a_kernel(slopes_ref, q_ref, k_ref, v_ref, o_ref, os_ref, ls_ref, bias_ref, *, seq):
    pair = pl.program_id(1)
    is_a = lax.broadcasted_iota(jnp.int32, (1, 128), 1) < A_HEAD_DIM

    row = lax.broadcasted_iota(jnp.int32, (2 * BAND, 2 * BAND), 0)
    col = lax.broadcasted_iota(jnp.int32, (2 * BAND, 2 * BAND), 1)
    rel = BAND + (row & (BAND - 1)) - col
    slope = jnp.where(row < BAND, slopes_ref[2 * pair], slopes_ref[2 * pair + 1])
    in_band = (rel >= 0) & (rel <= BAND)
    for pat, (_, dil) in enumerate(A_PATTERNS):
        bias_ref[pat] = jnp.where(in_band, -slope * (rel * dil).astype(F32), -jnp.inf)

    def block(pat, dil, q_start, k_start, first):
        nk = BAND if first else 2 * BAND
        q = q_ref[0, pl.ds(q_start, BAND, stride=dil), :] * (A_HEAD_DIM ** -0.5)
        k = k_ref[0, pl.ds(k_start, nk, stride=dil), :].astype(BF16)
        v = v_ref[0, pl.ds(k_start, nk, stride=dil), :].astype(BF16)
        q2 = jnp.concatenate([jnp.where(is_a, q, 0.0), jnp.where(is_a, 0.0, q)], axis=0).astype(BF16)
        s = _dot_nt(q2, k) + (bias_ref[pat, :, BAND:] if first else bias_ref[pat])
        m = jnp.max(s, axis=-1, keepdims=True)
        e = jnp.exp(s - m)
        l = jnp.sum(e, axis=-1, keepdims=True)
        o = _dot(e.astype(BF16), v) / l
        lse = m + jnp.log(l)
        os_ref[pat, pl.ds(q_start, BAND, stride=dil), :] = jnp.where(is_a, o[:BAND], o[BAND:])
        ls_ref[pat, pl.ds(q_start, BAND, stride=dil), :] = jnp.where(is_a, lse[:BAND], lse[BAND:])

    for pat, (_, dil) in enumerate(A_PATTERNS):
        nb = seq // dil // BAND

        def first_block(r, pat=pat, dil=dil):
            block(pat, dil, r, r, True)

        def later_block(t, pat=pat, dil=dil, nb=nb):
            r = t // (nb - 1)
            j = 1 + t % (nb - 1)
            block(pat, dil, r + dil * BAND * j, r + dil * BAND * (j - 1), False)

        _unrolled_loop(dil, first_block)
        if nb > 1:
            _unrolled_loop(dil * (nb - 1), later_block)

    rows = 256

    def combine(i, carry):
        sl = pl.ds(pl.multiple_of(i * rows, rows), rows)
        l0, l1, l2 = ls_ref[0, sl, :], ls_ref[1, sl, :], ls_ref[2, sl, :]
        mx = jnp.maximum(jnp.maximum(l0, l1), l2)
        w0, w1, w2 = jnp.exp(l0 - mx), jnp.exp(l1 - mx), jnp.exp(l2 - mx)
        num = w0 * os_ref[0, sl, :] + w1 * os_ref[1, sl, :] + w2 * os_ref[2, sl, :]
        o_ref[0, sl, :] = (num / (w0 + w1 + w2)).astype(o_ref.dtype)
        return carry

    lax.fori_loop(0, seq // rows, combine, 0)


def _attn_a_prompt(q, k, v, slopes):
    b, s, _ = q.shape
    assert s % (A_PATTERNS[-1][1] * BAND) == 0
    spec = pl.BlockSpec((1, s, 128), lambda i, p, *_: (i, 0, p))
    return pl.pallas_call(
        functools.partial(_attn_a_kernel, seq=s),
        grid_spec=pltpu.PrefetchScalarGridSpec(
            num_scalar_prefetch=1,
            grid=(b, A_WIDTH // 128),
            in_specs=[spec, spec, spec],
            out_specs=spec,
            scratch_shapes=[pltpu.VMEM((3, s, 128), F32), pltpu.VMEM((3, s, 128), F32),
                            pltpu.VMEM((3, 2 * BAND, 2 * BAND), F32)]),
        out_shape=jax.ShapeDtypeStruct((b, s, A_WIDTH), BF16),
        compiler_params=_params("parallel", "parallel"),
        name="attn_a_prompt",
    )(slopes, q, k, v)


def _pattern_multiplicity(delta):
    mult = jnp.zeros(delta.shape, F32)
    for window, dil in A_PATTERNS:
        hit = (delta >= 0) & (delta <= window)
        if dil > 1:
            hit = hit & ((delta & (dil - 1)) == 0)
        mult = mult + jnp.where(hit, 1.0, 0.0)
    return mult


def _attn_a_dec_body(slopes_ref, half, seq_in_group, q_ref, kn_ref, vn_ref, knt_ref, vnt_ref, ck_ref, cv_ref,
                     o_ref, nk_ref, nv_ref, *, nbuf, t_new):
    width = q_ref.shape[-1]
    heads = width // A_HEAD_DIM
    rows = heads * t_new
    q = q_ref[0]
    q_rows = jnp.concatenate([q] * heads, axis=0)
    row = lax.broadcasted_iota(jnp.int32, (rows, width), 0)
    lane = lax.broadcasted_iota(jnp.int32, (rows, width), 1)
    qm = jnp.where(row // t_new == lane // A_HEAD_DIM, q_rows, 0.0).astype(BF16)

    rcol = lax.broadcasted_iota(jnp.int32, (rows, 1), 0)
    slope = jnp.zeros((rows, 1), F32)
    for h in range(heads):
        slope = jnp.where(rcol // t_new == h, slopes_ref[half * heads + h], slope)
    tok = rcol % t_new

    s_c = _dot(qm, ck_ref[0].astype(BF16)) * (A_HEAD_DIM ** -0.5)
    pos = lax.broadcasted_iota(jnp.int32, (rows, nbuf), 1)
    delta_c = nbuf + tok - pos
    mult_c = _pattern_multiplicity(delta_c)
    s_c = jnp.where(mult_c > 0, s_c - slope * delta_c.astype(F32), -jnp.inf)
    qf = qm.astype(F32)
    knf = kn_ref[0].astype(BF16).astype(F32)
    vnf = vn_ref[0].astype(BF16).astype(F32)
    s_n, mult_n = [], []
    for t in range(t_new):
        delta = tok - t
        mult = _pattern_multiplicity(delta)
        sc = jnp.sum(qf * knf[t:t + 1, :], axis=-1, keepdims=True) * (A_HEAD_DIM ** -0.5)
        s_n.append(jnp.where(mult > 0, sc - slope * delta.astype(F32), -jnp.inf))
        mult_n.append(mult)
    m = jnp.max(s_c, axis=-1, keepdims=True)
    for sc in s_n:
        m = jnp.maximum(m, sc)
    e_c = jnp.exp(s_c - m) * mult_c
    l = jnp.sum(e_c, axis=-1, keepdims=True)
    acc = _dot_nt(e_c.astype(BF16), cv_ref[0].astype(BF16))
    for t in range(t_new):
        e = jnp.exp(s_n[t] - m) * mult_n[t]
        l = l + e
        acc = acc + e * vnf[t:t + 1, :]
    res = acc / l
    lane_head = lax.broadcasted_iota(jnp.int32, (t_new, width), 1) // A_HEAD_DIM
    out = jnp.zeros((t_new, width), F32)
    for h in range(heads):
        out = jnp.where(lane_head == h, res[h * t_new:(h + 1) * t_new], out)
    o_ref[0] = out.astype(o_ref.dtype)

    is_new = lax.broadcasted_iota(jnp.int32, (width, 128), 1) >= 128 - t_new
    to_tail = (128 - t_new - seq_in_group * t_new) % 128
    for c_ref, nt_ref, n_ref in ((ck_ref, knt_ref, nk_ref), (cv_ref, vnt_ref, nv_ref)):
        rolled = pltpu.roll(c_ref[0], nbuf - t_new, 1)
        n_ref[0, :, 0:nbuf - 128] = rolled[:, 0:nbuf - 128]
        n_ref[0, :, nbuf - 128:nbuf] = jnp.where(is_new, pltpu.roll(nt_ref[...], to_tail, 1),
                                                 rolled[:, nbuf - 128:nbuf])


def _retention_head(q, k, v, gate, rope_q, rope_k, dmask, qdec, kdec, cdec, g, state):
    (cos_q, sin_q), (cos_k, sin_k) = rope_q, rope_k
    qr = (q * cos_q + pltpu.roll(q, B_DIM // 2, 1) * sin_q).astype(BF16)
    kr = (k * cos_k + pltpu.roll(k, B_DIM // 2, 1) * sin_k) * (B_DIM ** -0.5)
    v = v.astype(BF16)
    att = _dot_nt(qr, kr.astype(BF16)) * dmask
    o = _dot(att.astype(BF16), v) + _dot(qr, state.astype(BF16)) * qdec
    kd = (kr * kdec).astype(BF16)
    new_state = state * cdec + lax.dot_general(kd, v, TN_DIMS, preferred_element_type=F32)
    d = o - jnp.mean(o, axis=-1, keepdims=True)
    y = d * lax.rsqrt(jnp.mean(d * d, axis=-1, keepdims=True) + EPS) * g
    return y * jax.nn.silu(gate), new_state


def _retention_kernel(q_ref, k_ref, v_ref, gate_ref, cos_ref, sin_ref, dmask_ref, qdec_ref, kdec_ref,
                      cdec_ref, g_ref, s0_ref, o_ref, s_ref, *, chunk, n_chunks):
    @pl.when(pl.program_id(1) == 0)
    def _():
        s_ref[...] = s0_ref[...]

    for c in range(n_chunks):
        rows = slice(c * chunk, (c + 1) * chunk)
        rope = (cos_ref[rows, :], sin_ref[rows, :])
        for h in range(B_HEADS):
            cols = slice(h * B_DIM, (h + 1) * B_DIM)
            y, s_ref[0, h] = _retention_head(
                q_ref[0, rows, cols], k_ref[0, rows, cols], v_ref[0, rows, cols], gate_ref[0, rows, cols],
                rope, rope, dmask_ref[h], qdec_ref[h], kdec_ref[h], cdec_ref[h], g_ref[:, cols], s_ref[0, h])
            o_ref[0, rows, cols] = y.astype(o_ref.dtype)


def _retention_dec_kernel(q_ref, k_ref, v_ref, gate_ref, cosq_ref, sinq_ref, cosk_ref, sink_ref, dmask_ref,
                          qdec_ref, kdec_ref, cdec_ref, g_ref, s0_ref, o_ref, s_ref, kpad, vpad, *, n_seq):
    t_rows = k_ref.shape[1]
    kpad[...] = jnp.zeros(kpad.shape, F32)
    vpad[...] = jnp.zeros(vpad.shape, F32)
    rope_q = (cosq_ref[...], sinq_ref[...])
    rope_k = (cosk_ref[...], sink_ref[...])
    for b in range(n_seq):
        kpad[b, 0:t_rows, :] = k_ref[b]
        vpad[b, 0:t_rows, :] = v_ref[b]
        for h in range(B_HEADS):
            cols = slice(h * B_DIM, (h + 1) * B_DIM)
            y, s_ref[b, h] = _retention_head(
                q_ref[b, :, cols], kpad[b, :, cols], vpad[b, :, cols], gate_ref[b, :, cols],
                rope_q, rope_k, dmask_ref[h], qdec_ref[h], kdec_ref[h], cdec_ref[h], g_ref[:, cols], s0_ref[b, h])
            o_ref[b, :, cols] = y.astype(o_ref.dtype)


def _retention_tables(chunk, q_pad, k_pad):
    one = np.float32(1.0)
    lg = np.log(one - np.float32(2.0) ** (np.float32(-5.0) - np.arange(B_HEADS, dtype=np.float32)))
    qi = np.arange(q_pad, dtype=np.float32)
    ki = np.arange(k_pad, dtype=np.float32)
    rel = qi[:, None] - ki[None, :]
    live = (rel >= 0) & (qi < chunk)[:, None] & (ki < chunk)[None, :]
    zero = np.float32(0.0)
    dmask = np.where(live, np.exp(np.maximum(rel, zero)[None] * lg[:, None, None]), zero)
    qdec = np.where((qi < chunk)[None, :], np.exp((qi + one)[None, :] * lg[:, None]), zero)
    kdec = np.where((ki < chunk)[None, :], np.exp((np.float32(chunk) - one - ki)[None, :] * lg[:, None]), zero)
    cdec = np.exp(np.float32(chunk) * lg)
    bc = lambda a: np.ascontiguousarray(np.broadcast_to(a[..., None], a.shape + (B_DIM,)), dtype=np.float32)
    return dmask.astype(np.float32), bc(qdec), bc(kdec), bc(cdec[:, None])


def _rope_tables(pos, dim, width):
    half = dim // 2
    inv = np.float32(ROPE_BASE) ** (-np.arange(half, dtype=np.float32) / np.float32(half))
    ang = (pos.astype(np.float32)[:, None] * inv[None, :]).astype(np.float32)
    cos, sin = np.cos(ang), np.sin(ang)
    zeros = np.zeros((pos.shape[0], width - dim), np.float32)
    return (np.concatenate([cos, cos, zeros], axis=1), np.concatenate([-sin, sin, zeros], axis=1))


def _retention_decode(q, k, v, gate, pos, g_ret, s0, n_seq):
    b, t, _ = q.shape
    t_pad = 8
    assert t <= t_pad and b % n_seq == 0
    pad_rows = lambda a: jnp.pad(a, ((0, 0), (0, t_pad - t), (0, 0)))
    cosk, sink = _rope_tables(np.pad(pos, (0, RET_CHUNK - t)), B_DIM, B_DIM)
    dmask, qdec, kdec, cdec = _retention_tables(t, t_pad, RET_CHUNK)
    tok = pl.BlockSpec((n_seq, t_pad, B_WIDTH), lambda i: (i, 0, 0))
    st = pl.BlockSpec((n_seq, B_HEADS, B_DIM, B_DIM), lambda i: (i, 0, 0, 0))
    consts = (cosk[:t_pad], sink[:t_pad], cosk, sink, dmask, qdec, kdec, cdec, g_ret.reshape(1, B_WIDTH))
    o, s_new = pl.pallas_call(
        functools.partial(_retention_dec_kernel, n_seq=n_seq),
        grid=(b // n_seq,),
        in_specs=[tok, tok, tok, tok] + [_resident(c.shape) for c in consts] + [st],
        out_specs=[tok, st],
        out_shape=[jax.ShapeDtypeStruct((b, t_pad, B_WIDTH), BF16),
                   jax.ShapeDtypeStruct((b, B_HEADS, B_DIM, B_DIM), F32)],
        scratch_shapes=[pltpu.VMEM((n_seq, RET_CHUNK, B_WIDTH), F32),
                        pltpu.VMEM((n_seq, RET_CHUNK, B_WIDTH), F32)],
        compiler_params=_params("parallel"),
        name="retention_decode",
    )(pad_rows(q), pad_rows(k), pad_rows(v), pad_rows(gate), *consts, s0)
    return o[:, :t], s_new


def _retention(q, k, v, gate, pos, g_ret, s0, rows):
    b, s, _ = q.shape
    cos, sin = _rope_tables(pos, B_DIM, B_DIM)
    dmask, qdec, kdec, cdec = _retention_tables(RET_CHUNK, RET_CHUNK, RET_CHUNK)
    tok = pl.BlockSpec((1, rows, B_WIDTH), lambda i, j: (i, j, 0))
    tab = pl.BlockSpec((rows, B_DIM), lambda i, j: (j, 0))
    st = pl.BlockSpec((1, B_HEADS, B_DIM, B_DIM), lambda i, j: (i, 0, 0, 0))
    return pl.pallas_call(
        functools.partial(_retention_kernel, chunk=RET_CHUNK, n_chunks=rows // RET_CHUNK),
        grid=(b, s // rows),
        in_specs=[tok, tok, tok, tok, tab, tab, _resident(dmask.shape), _resident(qdec.shape),
                  _resident(kdec.shape), _resident(cdec.shape), _resident((1, B_WIDTH)), st],
        out_specs=[tok, st],
        out_shape=[jax.ShapeDtypeStruct((b, s, B_WIDTH), BF16),
                   jax.ShapeDtypeStruct((b, B_HEADS, B_DIM, B_DIM), F32)],
        compiler_params=_params("parallel", "arbitrary"),
        name="retention",
    )(q, k, v, gate, cos, sin, dmask, qdec, kdec, cdec, g_ret.reshape(1, B_WIDTH), s0)


def _out_proj_kernel(a_ref, b_ref, w_ref, x_ref, y_ref):
    na = a_ref.shape[-1]
    y_ref[...] = x_ref[...] + _dot(a_ref[...], w_ref[:na, :]) + _dot(b_ref[...], w_ref[na:, :])


def _out_proj(a, b, w, x, tm):
    m, d = x.shape
    return pl.pallas_call(
        _out_proj_kernel,
        grid=(m // tm,),
        in_specs=[pl.BlockSpec((tm, a.shape[1]), lambda i: (i, 0)),
                  pl.BlockSpec((tm, b.shape[1]), lambda i: (i, 0)),
                  _resident(w.shape),
                  pl.BlockSpec((tm, d), lambda i: (i, 0))],
        out_specs=pl.BlockSpec((tm, d), lambda i: (i, 0)),
        out_shape=jax.ShapeDtypeStruct((m, d), F32),
        compiler_params=_params("parallel"),
        name="out_proj",
    )(a, b, w, x)


def _rope_group(x, cos, sin):
    return x * cos + (pltpu.roll(x, C_ROPE // 2, 1) + pltpu.roll(x, 128 - C_ROPE // 2, 1)) * sin


def _mla_prep_kernel(x_ref, g_ref, wdq_ref, gq_ref, wuqn_ref, wuqr_ref, wdkv_ref, gkv_ref, wukt_ref,
                     cos_ref, sin_ref, qcat_ref, kcat_ref, ckv_ref, kr_ref):
    h = _rms(x_ref[0], g_ref[...]).astype(BF16)
    cq = _rms(_dot(h, wdq_ref[...]), gq_ref[...]).astype(BF16)
    q_nope = _dot(cq, wuqn_ref[...])
    q_rope = _dot(cq, wuqr_ref[...])
    kv = _dot(h, wdkv_ref[...])
    cos = cos_ref[...]
    sin = sin_ref[...]
    c_kv = _rms(kv[:, :C_KV_LORA], gkv_ref[...])
    k_rope = _rope_group(kv[:, C_KV_LORA:], cos, sin)
    ckv_ref[0] = c_kv
    kr_ref[0] = k_rope.T[:C_ROPE, :]
    kcat_ref[0, :, :C_KV_LORA] = c_kv.astype(BF16)
    kcat_ref[0, :, C_KV_LORA:] = k_rope.astype(BF16)
    for hd in range(C_HEADS):
        cols = slice(hd * 128, (hd + 1) * 128)
        qcat_ref[0, hd, :, :C_KV_LORA] = _dot(q_nope[:, cols].astype(BF16), wukt_ref[hd]).astype(BF16)
        qcat_ref[0, hd, :, C_KV_LORA:] = _rope_group(q_rope[:, cols], cos, sin).astype(BF16)


def _mla_prep(x, pos, g, wdq, gq, wuqn, wuqr, wdkv, gkv, wukt, tm):
    b, s, d = x.shape
    cos, sin = _rope_tables(pos, C_ROPE, 128)
    tab = pl.BlockSpec((tm, 128), lambda i, j: (j, 0))
    return pl.pallas_call(
        _mla_prep_kernel,
        grid=(b, s // tm),
        in_specs=[pl.BlockSpec((1, tm, d), lambda i, j: (i, j, 0)),
                  _resident((1, d)), _resident(wdq.shape), _resident((1, C_Q_LORA)),
                  _resident(wuqn.shape), _resident(wuqr.shape), _resident(wdkv.shape),
                  _resident((1, C_KV_LORA)), _resident(wukt.shape), tab, tab],
        out_specs=[pl.BlockSpec((1, C_HEADS, tm, C_CAT), lambda i, j: (i, 0, j, 0)),
                   pl.BlockSpec((1, tm, C_CAT), lambda i, j: (i, j, 0)),
                   pl.BlockSpec((1, tm, C_KV_LORA), lambda i, j: (i, j, 0)),
                   pl.BlockSpec((1, C_ROPE, tm), lambda i, j: (i, 0, j))],
        out_shape=[jax.ShapeDtypeStruct((b, C_HEADS, s, C_CAT), BF16),
                   jax.ShapeDtypeStruct((b, s, C_CAT), BF16),
                   jax.ShapeDtypeStruct((b, s, C_KV_LORA), F32),
                   jax.ShapeDtypeStruct((b, C_ROPE, s), F32)],
        compiler_params=_params("parallel", "parallel"),
        name="mla_prep",
    )(x, g.reshape(1, d), wdq, gq.reshape(1, -1), wuqn, wuqr, wdkv, gkv.reshape(1, -1), wukt, cos, sin)


def _mla_out(o_lat_heads, wuv_ref, wo_ref, x):
    o = jnp.concatenate([_dot(o_lat_heads[hd].astype(BF16), wuv_ref[hd]) for hd in range(C_HEADS)], axis=1)
    return x + _dot(o.astype(BF16), wo_ref[...])


def _mla_flash_blocks(i, q_ref, k_ref, m_ref, l_ref, acc_ref, *, tq, tk):
    last = (i * tq + tq - 1) // tk
    c_exp = MLA_SCALE * 1.4426950408889634
    rows = 512
    group = rows // tq

    def key_block(j, masked, width=tk, first=False):
        k = k_ref[0, pl.ds(pl.multiple_of(j * tk, tk), width), :]
        n_chunks = width // 128
        if masked:
            qpos = i * tq + lax.broadcasted_iota(jnp.int32, (rows, 128), 0) % tq
            lane = lax.broadcasted_iota(jnp.int32, (rows, 128), 1)
        for g in range(C_HEADS // group):
            hs = slice(g * group, (g + 1) * group)
            s = _dot_nt(q_ref[0, hs].reshape(rows, C_CAT), k)
            chunks = [s[:, c * 128:(c + 1) * 128] for c in range(n_chunks)]
            if masked:
                chunks = [jnp.where(j * tk + c * 128 + lane <= qpos, sc, -jnp.inf)
                          for c, sc in enumerate(chunks)]
            row_max = jnp.max(functools.reduce(jnp.maximum, chunks), axis=-1, keepdims=True)
            if first:
                m_new = jnp.broadcast_to(row_max, (rows, 128))
            else:
                m_old = m_ref[hs].reshape(rows, 128)
                m_new = jnp.maximum(m_old, row_max)
                alpha = jnp.exp2((m_old - m_new) * c_exp)
            p = [jnp.exp2((sc - m_new) * c_exp) for sc in chunks]
            row_sum = jnp.sum(functools.reduce(jnp.add, p), axis=-1, keepdims=True)
            pv = _dot(jnp.concatenate(p, axis=1).astype(BF16), k[:, :C_KV_LORA])
            if first:
                l_new = jnp.broadcast_to(row_sum, (rows, 128))
                acc = pv
            else:
                l_new = alpha * l_ref[hs].reshape(rows, 128) + row_sum
                acc = (jnp.concatenate([alpha] * (C_KV_LORA // 128), axis=1)
                       * acc_ref[hs].reshape(rows, C_KV_LORA) + pv)
            m_ref[hs] = m_new.reshape(group, tq, 128)
            l_ref[hs] = l_new.reshape(group, tq, 128)
            acc_ref[hs] = acc.reshape(group, tq, C_KV_LORA)

    def body(j, carry):
        key_block(j, masked=False)
        return carry

    @pl.when(last > 0)
    def _():
        key_block(0, masked=False, first=True)
        lax.fori_loop(1, last, body, 0)

    ratio = tk // tq
    for r in range(ratio):
        @pl.when(i == r)
        def _(r=r):
            key_block(0, masked=True, width=(r + 1) * tq, first=True)

        @pl.when((i % ratio == r) & (i >= ratio))
        def _(r=r):
            key_block(last, masked=True, width=(r + 1) * tq)


def _mla_flash_finish(x_ref, wuv_ref, wo_ref, y_ref, l_ref, acc_ref):
    heads = []
    for hd in range(C_HEADS):
        inv = 1.0 / l_ref[hd]
        heads.append(acc_ref[hd] * jnp.concatenate([inv] * (C_KV_LORA // 128), axis=1))
    y_ref[0] = _mla_out(heads, wuv_ref, wo_ref, x_ref[0])


def _flash_decode_kernel(slopes_ref, q_ref, k_ref, x_ref, wuv_ref, wo_ref,
                         dq_ref, dkn_ref, dvn_ref, dknt_ref, dvnt_ref, dck_ref, dcv_ref,
                         y_ref, do_ref, dnk_ref, dnv_ref, m_ref, l_ref, acc_ref,
                         *, tq, tk, nbuf, t_new):
    i = pl.program_id(1)
    step = pl.program_id(0) * pl.num_programs(1) + i
    _mla_flash_blocks(i, q_ref, k_ref, m_ref, l_ref, acc_ref, tq=tq, tk=tk)
    _attn_a_dec_body(slopes_ref, step % 2, (step // 2) % (128 // t_new), dq_ref, dkn_ref, dvn_ref, dknt_ref,
                     dvnt_ref, dck_ref, dcv_ref, do_ref, dnk_ref, dnv_ref, nbuf=nbuf, t_new=t_new)
    _mla_flash_finish(x_ref, wuv_ref, wo_ref, y_ref, l_ref, acc_ref)


def _mla_flash_with_decode(qcat, kcat, x, wuv, wo, slopes, dq, dkn, dvn, dkn_t, dvn_t, cache_kt, cache_vt,
                           tq, tk):
    b, s, d = x.shape
    bs, t_new, _ = dq.shape
    nbuf = cache_kt.shape[2]
    nq = s // tq
    width = A_WIDTH // 2
    assert b * nq == 2 * bs and 128 % t_new == 0
    per_group = 128 // t_new
    seq = lambda i, j: (i * nq + j) // 2
    half = lambda i, j: (i * nq + j) % 2
    small = pl.BlockSpec((1, t_new, width), lambda i, j, *_: (seq(i, j), 0, half(i, j)))
    newt = pl.BlockSpec((width, 128), lambda i, j, *_: (half(i, j), seq(i, j) // per_group))
    big = pl.BlockSpec((1, width, nbuf), lambda i, j, *_: (seq(i, j), half(i, j), 0))
    return pl.pallas_call(
        functools.partial(_flash_decode_kernel, tq=tq, tk=tk, nbuf=nbuf, t_new=t_new),
        grid_spec=pltpu.PrefetchScalarGridSpec(
            num_scalar_prefetch=1,
            grid=(b, nq),
            in_specs=[pl.BlockSpec((1, C_HEADS, tq, C_CAT), lambda i, j, *_: (i, 0, j, 0)),
                      pl.BlockSpec((1, s, C_CAT), lambda i, j, *_: (i, 0, 0)),
                      pl.BlockSpec((1, tq, d), lambda i, j, *_: (i, j, 0)),
                      _resident(wuv.shape), _resident(wo.shape),
                      small, small, small, newt, newt, big, big],
            out_specs=[pl.BlockSpec((1, tq, d), lambda i, j, *_: (i, j, 0)), small, big, big],
            scratch_shapes=[pltpu.VMEM((C_HEADS, tq, 128), F32), pltpu.VMEM((C_HEADS, tq, 128), F32),
                            pltpu.VMEM((C_HEADS, tq, C_KV_LORA), F32)]),
        out_shape=[jax.ShapeDtypeStruct((b, s, d), F32),
                   jax.ShapeDtypeStruct((bs, t_new, A_WIDTH), BF16),
                   jax.ShapeDtypeStruct(cache_kt.shape, F32),
                   jax.ShapeDtypeStruct(cache_vt.shape, F32)],
        compiler_params=_params("parallel", "parallel"),
        name="mla_flash_with_decode",
    )(slopes, qcat, kcat, x, wuv, wo, dq, dkn, dvn, dkn_t, dvn_t, cache_kt, cache_vt)


def _mla_dec_kernel(pt_ref, q_ref, kn_ref, cpool_ref, rpool_ref, o_ref, cbuf, rbuf, sem,
                    *, n_pages, t_new, n_split):
    b = pl.program_id(0)
    slot = b % 2
    per_split = n_pages // n_split

    def page_copies(batch, sl, pg):
        pid = pt_ref[pg, batch]
        return (pltpu.make_async_copy(cpool_ref.at[pid], cbuf.at[sl, pg], sem.at[0, sl]),
                pltpu.make_async_copy(rpool_ref.at[pid], rbuf.at[sl, :, pg * PAGE:(pg + 1) * PAGE],
                                      sem.at[1, sl]))

    def start_batch(batch, sl):
        for pg in range(n_pages):
            for cp in page_copies(batch, sl, pg):
                cp.start()

    @pl.when(b == 0)
    def _():
        start_batch(0, 0)

    @pl.when(b + 1 < pl.num_programs(0))
    def _():
        start_batch(b + 1, 1 - slot)

    q = q_ref[0]
    rows = q.shape[0]
    q_lat = q[:, :C_KV_LORA]
    q_rope = q[:, C_KV_LORA:C_KV_LORA + C_ROPE]

    for pg in range(n_pages):
        for cp in page_copies(b, slot, pg):
            cp.wait()

    parts = []
    for part in range(n_split):
        keys = per_split * PAGE
        c_part = cbuf[slot, part * per_split:(part + 1) * per_split].reshape(keys, C_KV_LORA).astype(BF16)
        r_part = rbuf[slot, :, part * keys:(part + 1) * keys].astype(BF16)
        s = (_dot_nt(q_lat, c_part) + _dot(q_rope, r_part)) * MLA_SCALE
        m = jnp.max(s, axis=-1, keepdims=True)
        p = jnp.exp(s - m)
        parts.append((m, jnp.sum(p, axis=-1, keepdims=True), _dot(p.astype(BF16), c_part)))

    qf = q.astype(F32)
    knf = kn_ref[0].astype(F32)
    tok = lax.broadcasted_iota(jnp.int32, (rows, 1), 0) % t_new
    s_n = []
    for t in range(t_new):
        sc = jnp.sum(qf * knf[t:t + 1, :], axis=-1, keepdims=True) * MLA_SCALE
        s_n.append(jnp.where(tok >= t, sc, -jnp.inf))

    m = functools.reduce(jnp.maximum, [pm for pm, _, _ in parts] + s_n)
    l = jnp.zeros((rows, 1), F32)
    acc = jnp.zeros((rows, C_KV_LORA), F32)
    for part_m, part_l, part_acc in parts:
        w = jnp.exp(part_m - m)
        l = l + w * part_l
        acc = acc + w * part_acc
    for t in range(t_new):
        e = jnp.exp(s_n[t] - m)
        l = l + e
        acc = acc + e * knf[t:t + 1, :C_KV_LORA]
    o_ref[0] = (acc / l).astype(o_ref.dtype)


def _mla_decode(q_rows, k_new, c_pool, r_pool_t, page_table_t):
    b, rows, _ = q_rows.shape
    t_new = k_new.shape[1]
    n_pages = page_table_t.shape[0]
    n_split = 4 if n_pages % 4 == 0 else 1
    return pl.pallas_call(
        functools.partial(_mla_dec_kernel, n_pages=n_pages, t_new=t_new, n_split=n_split),
        grid_spec=pltpu.PrefetchScalarGridSpec(
            num_scalar_prefetch=1,
            grid=(b,),
            in_specs=[pl.BlockSpec((1, rows, C_CAT), lambda i, *_: (i, 0, 0)),
                      pl.BlockSpec((1, t_new, C_CAT), lambda i, *_: (i, 0, 0)),
                      pl.BlockSpec(memory_space=pl.ANY),
                      pl.BlockSpec(memory_space=pl.ANY)],
            out_specs=pl.BlockSpec((1, rows, C_KV_LORA), lambda i, *_: (i, 0, 0)),
            scratch_shapes=[pltpu.VMEM((2, n_pages, PAGE, C_KV_LORA), F32),
                            pltpu.VMEM((2, C_ROPE, n_pages * PAGE), F32),
                            pltpu.SemaphoreType.DMA((2, 2))]),
        out_shape=jax.ShapeDtypeStruct((b, rows, C_KV_LORA), BF16),
        compiler_params=_params("arbitrary"),
        name="mla_decode",
    )(page_table_t, q_rows, k_new, c_pool, r_pool_t)


def _mla_out_kernel(o_ref, wuv_ref, wo_ref, x_ref, y_ref):
    y_ref[...] = _mla_out([o_ref[hd] for hd in range(C_HEADS)], wuv_ref, wo_ref, x_ref[...])


def _mla_out_proj(o_heads, wuv, wo, x):
    m, d = x.shape
    return pl.pallas_call(
        _mla_out_kernel,
        grid=(1,),
        in_specs=[_resident(o_heads.shape), _resident(wuv.shape), _resident(wo.shape), _resident(x.shape)],
        out_specs=pl.BlockSpec((m, d), lambda i: (0, 0)),
        out_shape=jax.ShapeDtypeStruct((m, d), F32),
        compiler_params=_params("arbitrary"),
        name="mla_out_proj",
    )(o_heads, wuv, wo, x)


def _ffn_kernel(*refs, tm, tiles_per_seq, seq_in_tile, final_norm):
    x_ref, g_ref, wup_ref, wconv_ref, bconv_ref, wdown_ref = refs[:6]
    refs = refs[6:]
    if seq_in_tile:
        e1_ref, e2_ref = refs[:2]
        refs = refs[2:]
    if final_norm:
        gfin_ref = refs[0]
        refs = refs[1:]
    y_ref, gate_out_ref, gbuf = refs
    halo = 8

    if seq_in_tile:
        gbuf[0:halo, :] = jnp.zeros((halo, D_FF), F32)
    else:
        @pl.when(pl.program_id(0) % tiles_per_seq == 0)
        def _():
            gbuf[0:halo, :] = jnp.zeros((halo, D_FF), F32)

    x = x_ref[...]
    h = _rms(x, g_ref[...]).astype(BF16)
    gate = _dot(h, wup_ref[:, :D_FF])
    up = _dot(h, wup_ref[:, D_FF:])
    gbuf[halo:halo + tm, :] = gate
    tap0 = gbuf[halo - 2:halo - 2 + tm, :]
    tap1 = gbuf[halo - 1:halo - 1 + tm, :]
    if seq_in_tile:
        tok = lax.broadcasted_iota(jnp.int32, (tm, 1), 0) % seq_in_tile
        tap0 = jnp.where(tok < 2, e2_ref[...], tap0)
        tap1 = jnp.where(tok < 1, e1_ref[...], tap1)
        gate_out_ref[...] = gate
    else:
        gate_out_ref[0] = gate[tm - halo:, :]
        gbuf[0:halo, :] = gate[tm - halo:, :]
    conv = tap0 * wconv_ref[0:1, :] + bconv_ref[...] + tap1 * wconv_ref[1:2, :] + gate * wconv_ref[2:3, :]
    act = (jax.nn.silu(conv) * up).astype(BF16)
    y = x + _dot(act, wdown_ref[...])
    if final_norm:
        y = _rms(y, gfin_ref[...])
    y_ref[...] = y


def _conv_ffn(x, g, wup, wconv, bconv, wdown, layer, *, tm, seq, conv_state=None, g_final=None):
    m, d = x.shape
    seq_in_tile = 0 if conv_state is None else seq
    row = pl.BlockSpec((tm, d), lambda i: (i, 0))
    wide = pl.BlockSpec((tm, D_FF), lambda i: (i, 0))
    of_layer = lambda a: pl.BlockSpec((None,) + a.shape[1:], lambda i: (layer, 0, 0), pipeline_mode=pl.Buffered(1))
    args = [x, g.reshape(1, d), wup, wconv, bconv.reshape(1, D_FF), wdown]
    in_specs = [row, _resident((1, d)), of_layer(wup), _resident(wconv.shape),
                _resident((1, D_FF)), of_layer(wdown)]
    if conv_state is not None:
        assert tm % seq == 0 and seq >= 2
        zeros = jnp.zeros((m // seq, seq - 2, D_FF), F32)
        e2 = jnp.concatenate([conv_state, zeros], axis=1).reshape(m, D_FF)
        e1 = jnp.concatenate([conv_state[:, 1:], zeros, zeros[:, :1]], axis=1).reshape(m, D_FF)
        args += [e1, e2]
        in_specs += [wide, wide]
        gate_spec = wide
        gate_shape = jax.ShapeDtypeStruct((m, D_FF), F32)
    else:
        assert seq % tm == 0
        gate_spec = pl.BlockSpec((1, 8, D_FF), lambda i: (i // (seq // tm), 0, 0))
        gate_shape = jax.ShapeDtypeStruct((m // seq, 8, D_FF), F32)
    if g_final is not None:
        args.append(g_final.reshape(1, d))
        in_specs.append(_resident((1, d)))
    return pl.pallas_call(
        functools.partial(_ffn_kernel, tm=tm, tiles_per_seq=max(seq // tm, 1), seq_in_tile=seq_in_tile,
                          final_norm=g_final is not None),
        grid=(m // tm,),
        in_specs=in_specs,
        out_specs=[row, gate_spec],
        out_shape=[jax.ShapeDtypeStruct((m, d), F32), gate_shape],
        scratch_shapes=[pltpu.VMEM((tm + 8, D_FF), F32)],
        compiler_params=_params("arbitrary"),
        name="conv_ffn",
    )(*args)


def _prepare_weights(w_in_ab, w_out_ab, w_dq, w_uq, w_dkv, w_uk, w_uv, w_o_c, w_up, w_down):
    per_head = C_NOPE + C_ROPE
    uq = w_uq[0].reshape(C_Q_LORA, C_HEADS, per_head)
    wuqr = jnp.pad(uq[:, :, C_NOPE:], ((0, 0), (0, 0), (0, 128 - C_ROPE)))
    return dict(
        w_in=w_in_ab[0].astype(BF16),
        w_out=w_out_ab[0].astype(BF16),
        wdq=w_dq[0].astype(BF16),
        wuqn=uq[:, :, :C_NOPE].reshape(C_Q_LORA, C_HEADS * C_NOPE).astype(BF16),
        wuqr=wuqr.reshape(C_Q_LORA, C_HEADS * 128).astype(BF16),
        wdkv=jnp.pad(w_dkv[0], ((0, 0), (0, 128 - C_ROPE))).astype(BF16),
        wukt=w_uk[0].transpose(1, 2, 0).astype(BF16),
        wuv=w_uv[0].transpose(1, 0, 2).astype(BF16),
        wo=w_o_c[0].astype(BF16),
        w_up=w_up.astype(BF16),
        w_down=w_down.astype(BF16),
    )


def _alibi_slopes():
    heads = np.arange(A_HEADS, dtype=np.float32) + np.float32(1.0)
    return np.float32(2.0) ** (np.float32(-8.0) * heads / np.float32(A_HEADS))


AB_WIDTHS = (A_WIDTH,) * 3 + (B_WIDTH,) * 4


def kernel(x_prompt, x_sample, cache_a_k, cache_a_v, state_ret, cache_c_kv, cache_k_rope, state_ffn_conv,
           page_table, g_mix, g_ffn, g_final, w_in_ab, w_out_ab, g_ret, w_dq, g_q, w_uq, w_dkv, g_kv,
           w_uk, w_uv, w_o_c, w_up, w_conv, b_conv, w_down):
    assert g_mix.shape[0] == 2 and w_in_ab.shape[0] == 1 and w_dq.shape[0] == 1
    w = _prepare_weights(w_in_ab, w_out_ab, w_dq, w_uq, w_dkv, w_uk, w_uv, w_o_c, w_up, w_down)
    slopes = _alibi_slopes()
    tm_ffn = 512

    def ffn(x, layer, **kw):
        return _conv_ffn(x, g_ffn[layer], w["w_up"], w_conv[layer], b_conv[layer], w["w_down"], layer, **kw)

    def mla_prep(x3, pos, tm):
        return _mla_prep(x3, pos, g_mix[1], w["wdq"], g_q[0], w["wuqn"], w["wuqr"], w["wdkv"], g_kv[0],
                         w["wukt"], tm=tm)

    b, s, d = x_prompt.shape
    m = b * s
    pos_p = np.arange(s)
    qa, ka, va, qb, kb, vb, gb, ka_t, va_t = _norm_proj(x_prompt, g_mix[0], w["w_in"], AB_WIDTHS, tm=512,
                                                        transposed=(1, 2))
    o_a = _attn_a_prompt(qa, ka, va, slopes)
    o_b, ret_p = _retention(qb, kb, vb, gb, pos_p, g_ret[0], jnp.zeros((b, B_HEADS, B_DIM, B_DIM), F32), rows=512)
    xp = _out_proj(o_a.reshape(m, -1), o_b.reshape(m, -1), w["w_out"], x_prompt.reshape(m, d), tm=1024)
    xp, tails0 = ffn(xp, 0, tm=tm_ffn, seq=s)
    qcat_p, kcat_p, c_kv_p, k_rope_t_p = mla_prep(xp.reshape(b, s, d), pos_p, 1024)

    bs, t, _ = x_sample.shape
    ms = bs * t
    pos_s = PAST_LEN + np.arange(t)
    xs = x_sample.reshape(ms, d)
    sq, sk, sv, sqb, skb, svb, sgb, sk_t, sv_t = _norm_proj(xs.reshape(1, ms, d), g_mix[0], w["w_in"], AB_WIDTHS,
                                                            tm=ms, transposed=(1, 2))
    per_seq = lambda a: a.reshape(bs, t, -1)
    nbuf = cache_a_k.shape[2]
    feature_major = lambda a: a[0].transpose(0, 2, 3, 1).reshape(bs, A_WIDTH, nbuf)

    xp, so_a, new_k_s, new_v_s = _mla_flash_with_decode(
        qcat_p, kcat_p, xp.reshape(b, s, d), w["wuv"], w["wo"], slopes, per_seq(sq), per_seq(sk), per_seq(sv),
        sk_t[0], sv_t[0], feature_major(cache_a_k), feature_major(cache_a_v), tq=128, tk=512)
    y_prompt, tails1 = ffn(xp.reshape(m, d), 1, tm=tm_ffn, seq=s, g_final=g_final)

    so_b, ret_s = _retention_decode(per_seq(sqb), per_seq(skb), per_seq(svb), per_seq(sgb), pos_s, g_ret[0],
                                    state_ret[0], n_seq=4)
    xs = _out_proj(so_a.reshape(ms, -1), so_b.reshape(ms, -1), w["w_out"], xs, tm=ms)
    xs, gate0 = ffn(xs, 0, tm=ms, seq=t, conv_state=state_ffn_conv[0])
    qcat_s, kcat_s, c_kv_s, k_rope_t_s = mla_prep(xs.reshape(1, ms, d), np.tile(pos_s, bs), ms)
    q_rows = qcat_s.reshape(C_HEADS, bs, t, C_CAT).transpose(1, 0, 2, 3).reshape(bs, C_HEADS * t, C_CAT)
    o_lat = _mla_decode(q_rows, kcat_s.reshape(bs, t, C_CAT), cache_c_kv[0], cache_k_rope[0].transpose(0, 2, 1),
                        page_table.T)
    o_heads = o_lat.reshape(bs, C_HEADS, t, C_KV_LORA).transpose(1, 0, 2, 3).reshape(C_HEADS, ms, C_KV_LORA)
    xs = _mla_out_proj(o_heads, w["wuv"], w["wo"], xs)
    y_sample, gate1 = ffn(xs, 1, tm=ms, seq=t, conv_state=state_ffn_conv[1], g_final=g_final)

    keep = min(A_PATTERNS[-1][0], s)
    heads_p = lambda a: a[:, :, s - keep:].reshape(b, A_HEADS, A_HEAD_DIM, keep).transpose(0, 3, 1, 2)[None]
    heads_s = lambda a: a.reshape(bs, A_HEADS, A_HEAD_DIM, nbuf).transpose(0, 3, 1, 2)[None]
    last2 = lambda g: g.reshape(bs, t, D_FF)[:, t - 2:]
    return (y_prompt.reshape(b, s, d), y_sample.reshape(bs, t, d),
            heads_p(ka_t), heads_p(va_t), ret_p[None], c_kv_p[None], k_rope_t_p.transpose(0, 2, 1)[None],
            jnp.stack([tails0[:, 6:8], tails1[:, 6:8]]),
            heads_s(new_k_s), heads_s(new_v_s), ret_s[None], c_kv_s.reshape(1, bs, t, C_KV_LORA),
            k_rope_t_s.reshape(C_ROPE, bs, t).transpose(1, 2, 0)[None], jnp.stack([last2(gate0), last2(gate1)]))
```
